```python
import math
import jax, jax.numpy as jnp
from jax import lax
import numpy as np

D_MODEL = 1024
BATCH = 4
SEQ = 4096
DEPTH = 1

HEAD_DIM = 64
N_DIFF_HEADS = 8
DIFF_QK_DIM = HEAD_DIM // 2
DIFF_WIDTH = N_DIFF_HEADS * HEAD_DIM
N_SGU_HEADS = 8
SGU_WIDTH = N_SGU_HEADS * HEAD_DIM
MIX_WIDTH = DIFF_WIDTH + SGU_WIDTH
IN_COLS = 3 * DIFF_WIDTH + 2 * SGU_WIDTH
SGU_CHUNK = 128
Q_BLOCK = 128
ROPE_THETA = 10000.0
PEER_HEADS = 8
PEER_N_KEYS = 128
PEER_N_EXPERTS = PEER_N_KEYS * PEER_N_KEYS
PEER_TOPK = 16
PEER_QUERY_DIM = 256
PEER_HALF = PEER_QUERY_DIM // 2
PEER_CHUNK = 128
NORM_EPS = 1e-6
LN_EPS = 1e-5

kernel_name = 'hymba_diffattn_sgu_peer_block'


def rmsnorm(x, w):
    xf = x.astype(jnp.float32)
    y = xf * lax.rsqrt(jnp.mean(xf * xf, axis=-1, keepdims=True) + NORM_EPS)
    return (y * w.astype(jnp.float32)).astype(x.dtype)


def layernorm(x, w, b):
    xf = x.astype(jnp.float32)
    mu = jnp.mean(xf, axis=-1, keepdims=True)
    xc = xf - mu
    y = xc * lax.rsqrt(jnp.mean(xc * xc, axis=-1, keepdims=True) + LN_EPS)
    return (y * w.astype(jnp.float32) + b.astype(jnp.float32)).astype(x.dtype)


def apply_rope(t, cos, sin):
    half = t.shape[-1] // 2
    t1, t2 = t[..., :half], t[..., half:]
    return jnp.concatenate([t1 * cos - t2 * sin, t2 * cos + t1 * sin], axis=-1)


def diff_attention(q1, q2, k1, k2, v, lam):
    B, H, S, d = q1.shape
    nb = S // Q_BLOCK
    scale = d ** -0.5
    q1b = q1.reshape(B, H, nb, Q_BLOCK, d).transpose(2, 0, 1, 3, 4)
    q2b = q2.reshape(B, H, nb, Q_BLOCK, d).transpose(2, 0, 1, 3, 4)
    starts = jnp.arange(nb, dtype=jnp.int32) * Q_BLOCK
    kpos = jnp.arange(S, dtype=jnp.int32)

    def one_block(args):
        qa, qb, start = args
        qpos = start + jnp.arange(Q_BLOCK, dtype=jnp.int32)
        mask = kpos[None, :] <= qpos[:, None]
        s1 = jnp.einsum('bhqd,bhkd->bhqk', qa, k1).astype(jnp.float32) * scale
        s2 = jnp.einsum('bhqd,bhkd->bhqk', qb, k2).astype(jnp.float32) * scale
        p1 = jax.nn.softmax(jnp.where(mask, s1, -jnp.inf), axis=-1)
        p2 = jax.nn.softmax(jnp.where(mask, s2, -jnp.inf), axis=-1)
        a = (p1 - lam * p2).astype(v.dtype)
        return jnp.einsum('bhqk,bhkd->bhqd', a, v)

    out = lax.map(one_block, (q1b, q2b, starts))
    return out.transpose(1, 0, 3, 2, 4).reshape(B, S, H, v.shape[-1])


def spatial_gating(u, vg, ln_w, ln_b, ws, sb):
    B, S, _ = u.shape
    nc = S // SGU_CHUNK
    vg = layernorm(vg, ln_w, ln_b).reshape(B, nc, SGU_CHUNK, N_SGU_HEADS, HEAD_DIM)
    causal = jnp.tril(jnp.ones((SGU_CHUNK, SGU_CHUNK), dtype=bool))
    w = jnp.where(causal[None], ws, jnp.zeros_like(ws))
    mixed = jnp.einsum('hts,bcshd->bcthd', w, vg) + sb.T[None, None, :, :, None]
    out = u.reshape(B, nc, SGU_CHUNK, N_SGU_HEADS, HEAD_DIM) * mixed
    return out.reshape(B, S, N_SGU_HEADS, HEAD_DIM)


def peer_ffn(x, wq, keys, u_tab, v_tab):
    B, S, D = x.shape
    T = B * S
    xt = x.reshape(T, D)
    q = (xt @ wq).reshape(T, PEER_HEADS, 2, PEER_HALF)
    s1 = jnp.einsum('thd,nd->thn', q[:, :, 0], keys[0]).astype(jnp.float32)
    s2 = jnp.einsum('thd,nd->thn', q[:, :, 1], keys[1]).astype(jnp.float32)
    v1, i1 = lax.top_k(s1, PEER_TOPK)
    v2, i2 = lax.top_k(s2, PEER_TOPK)
    cand = (v1[..., :, None] + v2[..., None, :]).reshape(T, PEER_HEADS, PEER_TOPK * PEER_TOPK)
    sc, ci = lax.top_k(cand, PEER_TOPK)
    e1 = jnp.take_along_axis(i1, ci // PEER_TOPK, axis=-1)
    e2 = jnp.take_along_axis(i2, ci % PEER_TOPK, axis=-1)
    idx = e1 * PEER_N_KEYS + e2
    g = jax.nn.softmax(sc, axis=-1)
    nch = T // PEER_CHUNK

    def one_chunk(args):
        xc, ic, gc = args
        uc = jnp.take(u_tab, ic, axis=0)
        act = jax.nn.gelu(jnp.einsum('chkd,cd->chk', uc, xc).astype(jnp.float32), approximate=False)
        coef = (gc * act).astype(xc.dtype)
        vc = jnp.take(v_tab, ic, axis=0)
        return jnp.einsum('chk,chkd->cd', coef, vc)

    y = lax.map(one_chunk, (xt.reshape(nch, PEER_CHUNK, D),
                            idx.reshape(nch, PEER_CHUNK, PEER_HEADS, PEER_TOPK),
                            g.reshape(nch, PEER_CHUNK, PEER_HEADS, PEER_TOPK)))
    return y.reshape(B, S, D)


def setup_inputs(seed: int = 0) -> dict:
    key = jax.random.key(seed)
    ks = jax.random.split(key, 24)
    f32 = jnp.float32
    nrm = lambda k, shape, s: jax.random.normal(k, shape, f32) * s
    gain = lambda k, shape: 1.0 + 0.02 * jax.random.normal(k, shape, f32)
    return {
        'x': jax.random.normal(ks[0], (BATCH, SEQ, D_MODEL), f32),
        'attn_norm_w': gain(ks[1], (DEPTH, D_MODEL)),
        'w_in': nrm(ks[2], (DEPTH, D_MODEL, IN_COLS), D_MODEL ** -0.5),
        'lambda_q1': nrm(ks[3], (DEPTH, DIFF_QK_DIM), 0.1),
        'lambda_k1': nrm(ks[4], (DEPTH, DIFF_QK_DIM), 0.1),
        'lambda_q2': nrm(ks[5], (DEPTH, DIFF_QK_DIM), 0.1),
        'lambda_k2': nrm(ks[6], (DEPTH, DIFF_QK_DIM), 0.1),
        'subln_w': gain(ks[7], (DEPTH, HEAD_DIM)),
        'sgu_ln_w': gain(ks[8], (DEPTH, SGU_WIDTH)),
        'sgu_ln_b': nrm(ks[9], (DEPTH, SGU_WIDTH), 0.02),
        'sgu_ws': nrm(ks[10], (DEPTH, N_SGU_HEADS, SGU_CHUNK, SGU_CHUNK), SGU_CHUNK ** -0.5),
        'sgu_b': gain(ks[11], (DEPTH, N_SGU_HEADS, SGU_CHUNK)),
        'sgu_out_norm_w': gain(ks[12], (DEPTH, SGU_WIDTH)),
        'w_out': nrm(ks[13], (DEPTH, MIX_WIDTH, D_MODEL), MIX_WIDTH ** -0.5),
        'ffn_norm_w': gain(ks[14], (DEPTH, D_MODEL)),
        'peer_wq': nrm(ks[15], (DEPTH, D_MODEL, PEER_HEADS * PEER_QUERY_DIM), D_MODEL ** -0.5),
        'peer_keys': nrm(ks[16], (DEPTH, 2, PEER_N_KEYS, PEER_HALF), PEER_HALF ** -0.5),
        'peer_u': nrm(ks[17], (DEPTH, PEER_N_EXPERTS, D_MODEL), D_MODEL ** -0.5),
        'peer_v': nrm(ks[18], (DEPTH, PEER_N_EXPERTS, D_MODEL), 0.5),
        'final_norm_w': gain(ks[19], (D_MODEL,)),
    }


def reference(x, attn_norm_w, w_in, lambda_q1, lambda_k1, lambda_q2, lambda_k2, subln_w,
              sgu_ln_w, sgu_ln_b, sgu_ws, sgu_b, sgu_out_norm_w, w_out, ffn_norm_w,
              peer_wq, peer_keys, peer_u, peer_v, final_norm_w):
    B, S, _ = x.shape
    pos = jnp.arange(S, dtype=jnp.float32)
    inv_freq = ROPE_THETA ** (-jnp.arange(0, DIFF_QK_DIM, 2, dtype=jnp.float32) / DIFF_QK_DIM)
    ang = pos[:, None] * inv_freq[None, :]
    cos = jnp.cos(ang).astype(x.dtype)
    sin = jnp.sin(ang).astype(x.dtype)
    splits = [DIFF_WIDTH, 2 * DIFF_WIDTH, 3 * DIFF_WIDTH, 3 * DIFF_WIDTH + SGU_WIDTH]
    for l in range(DEPTH):
        h = rmsnorm(x, attn_norm_w[l])
        proj = h @ w_in[l]
        q, k, v, zu, zv = jnp.split(proj, splits, axis=-1)
        q = apply_rope(q.reshape(B, S, N_DIFF_HEADS, 2, DIFF_QK_DIM).transpose(0, 2, 3, 1, 4), cos, sin)
        k = apply_rope(k.reshape(B, S, N_DIFF_HEADS, 2, DIFF_QK_DIM).transpose(0, 2, 3, 1, 4), cos, sin)
        v = v.reshape(B, S, N_DIFF_HEADS, HEAD_DIM).transpose(0, 2, 1, 3)
        lam_init = 0.8 - 0.6 * math.exp(-0.3 * l)
        lam = (jnp.exp(jnp.sum(lambda_q1[l] * lambda_k1[l]).astype(jnp.float32))
               - jnp.exp(jnp.sum(lambda_q2[l] * lambda_k2[l]).astype(jnp.float32)) + lam_init)
        attn = diff_attention(q[:, :, 0], q[:, :, 1], k[:, :, 0], k[:, :, 1], v, lam)
        attn = (rmsnorm(attn, subln_w[l]) * (1.0 - lam_init)).reshape(B, S, DIFF_WIDTH)
        sgu = spatial_gating(jax.nn.gelu(zu, approximate=False), jax.nn.gelu(zv, approximate=False),
                             sgu_ln_w[l], sgu_ln_b[l], sgu_ws[l], sgu_b[l])
        sgu = rmsnorm(sgu, sgu_out_norm_w[l].reshape(N_SGU_HEADS, HEAD_DIM)).reshape(B, S, SGU_WIDTH)
        x = x + jnp.concatenate([attn, sgu], axis=-1) @ w_out[l]
        x = x + peer_ffn(rmsnorm(x, ffn_norm_w[l]), peer_wq[l], peer_keys[l], peer_u[l], peer_v[l])
    return rmsnorm(x, final_norm_w)
```

```python
import functools
import math

import jax
import jax.numpy as jnp
from jax import lax
from jax.experimental import pallas as pl
from jax.experimental.pallas import tpu as pltpu

F32 = jnp.float32
BF16 = jnp.bfloat16

D_MODEL = 1024
HEAD_DIM = 64
N_DIFF_HEADS = 8
DIFF_QK_DIM = HEAD_DIM // 2
DIFF_WIDTH = N_DIFF_HEADS * HEAD_DIM
N_SGU_HEADS = 8
SGU_WIDTH = N_SGU_HEADS * HEAD_DIM
IN_COLS = 3 * DIFF_WIDTH + 2 * SGU_WIDTH
SGU_CHUNK = 128
ROPE_THETA = 10000.0
PEER_HEADS = 8
PEER_N_KEYS = 128
PEER_N_EXPERTS = PEER_N_KEYS * PEER_N_KEYS
PEER_TOPK = 16
PEER_QUERY_DIM = 256
PEER_HALF = PEER_QUERY_DIM // 2
NORM_EPS = 1e-6
LN_EPS = 1e-5

LANES = 128
SLAB_HEADS = LANES // HEAD_DIM
N_SLABS = DIFF_WIDTH // LANES
V7X_VMEM_LIMIT_BYTES = 56 * 1024 * 1024
NOT_SELECTED = 99.0
NEG_INF = float("-inf")
SQRT_HALF = math.sqrt(0.5)


def _gelu(z):
    return 0.5 * z * (1.0 + lax.erf(z * SQRT_HALF))


def _rms(x, w):
    return x * lax.rsqrt(jnp.mean(x * x, axis=-1, keepdims=True) + NORM_EPS) * w


def _group_rms(o, w, low_half):
    ss = o * o
    s_lo = jnp.sum(jnp.where(low_half, ss, 0.0), axis=-1, keepdims=True)
    s_hi = jnp.sum(jnp.where(low_half, 0.0, ss), axis=-1, keepdims=True)
    ms = jnp.where(low_half, s_lo, s_hi) * (1.0 / HEAD_DIM)
    return o * lax.rsqrt(ms + NORM_EPS) * w


def _in_proj_kernel(x_ref, nw_ref, win_ref, cos_ref, sin_ref, lnw_ref, lnb_ref, ws_ref, sb_ref, onw_ref,
                    q_ref, k_ref, v_ref, g_ref, *, tm):
    x = x_ref[...]
    h = _rms(x, nw_ref[...])
    proj = jnp.dot(h.astype(BF16), win_ref[...], preferred_element_type=F32)

    cos = cos_ref[...]
    sin = sin_ref[...]
    lane = lax.broadcasted_iota(jnp.int32, (tm, LANES), 1)
    first_half = (lane & (DIFF_QK_DIM // 2)) == 0

    def rope(t):
        partner = jnp.where(first_half, pltpu.roll(t, LANES - DIFF_QK_DIM // 2, 1),
                            pltpu.roll(t, DIFF_QK_DIM // 2, 1))
        return t * cos + partner * sin

    qk_scale = DIFF_QK_DIM ** -0.5
    for c in range(N_SLABS):
        sl = slice(c * LANES, (c + 1) * LANES)
        q_ref[:, sl] = (rope(proj[:, sl]) * qk_scale).astype(BF16)
        k_ref[:, sl] = rope(proj[:, DIFF_WIDTH + c * LANES:DIFF_WIDTH + (c + 1) * LANES]).astype(BF16)
    v_ref[...] = proj[:, 2 * DIFF_WIDTH:3 * DIFF_WIDTH].astype(BF16)

    u = _gelu(proj[:, 3 * DIFF_WIDTH:3 * DIFF_WIDTH + SGU_WIDTH])
    vg = _gelu(proj[:, 3 * DIFF_WIDTH + SGU_WIDTH:])
    mu = jnp.mean(vg, axis=-1, keepdims=True)
    xc = vg - mu
    vgn = (xc * lax.rsqrt(jnp.mean(xc * xc, axis=-1, keepdims=True) + LN_EPS) * lnw_ref[...]
           + lnb_ref[...]).astype(BF16)

    row = lax.broadcasted_iota(jnp.int32, (SGU_CHUNK, SGU_CHUNK), 0)
    col = lax.broadcasted_iota(jnp.int32, (SGU_CHUNK, SGU_CHUNK), 1)
    causal = row >= col
    w_heads = [jnp.where(causal, ws_ref[hh], 0.0).astype(BF16) for hh in range(N_SGU_HEADS)]
    low_half = lax.broadcasted_iota(jnp.int32, (SGU_CHUNK, LANES), 1) < HEAD_DIM

    for ch in range(tm // SGU_CHUNK):
        rows = slice(ch * SGU_CHUNK, (ch + 1) * SGU_CHUNK)
        for c in range(SGU_WIDTH // LANES):
            sl = slice(c * LANES, (c + 1) * LANES)
            vs = vgn[rows, sl]
            r_lo = jnp.dot(w_heads[SLAB_HEADS * c], vs, preferred_element_type=F32)
            r_hi = jnp.dot(w_heads[SLAB_HEADS * c + 1], vs, preferred_element_type=F32)
            mixed = jnp.where(low_half, r_lo, r_hi) + sb_ref[:, sl]
            o = u[rows, sl] * mixed
            g_ref[rows, sl] = _group_rms(o, onw_ref[:, sl], low_half).astype(BF16)


def _in_proj(x2, nw, win, cos_t, sin_t, lnw, lnb, ws, sb, onw, *, seq, tm):
    T = x2.shape[0]
    n_seq_tiles = seq // tm
    full = lambda shape: pl.BlockSpec(shape, lambda i: (0,) * len(shape))
    out = jax.ShapeDtypeStruct((T, DIFF_WIDTH), BF16)
    return pl.pallas_call(
        functools.partial(_in_proj_kernel, tm=tm),
        grid=(T // tm,),
        in_specs=[
            pl.BlockSpec((tm, D_MODEL), lambda i: (i, 0)),
            full((1, D_MODEL)),
            full((D_MODEL, IN_COLS)),
            pl.BlockSpec((tm, LANES), lambda i: (i % n_seq_tiles, 0)),
            pl.BlockSpec((tm, LANES), lambda i: (i % n_seq_tiles, 0)),
            full((1, SGU_WIDTH)),
            full((1, SGU_WIDTH)),
            full((N_SGU_HEADS, SGU_CHUNK, SGU_CHUNK)),
            full((SGU_CHUNK, SGU_WIDTH)),
            full((1, SGU_WIDTH)),
        ],
        out_specs=[pl.BlockSpec((tm, DIFF_WIDTH), lambda i: (i, 0))] * 4,
        out_shape=[out] * 4,
        compiler_params=pltpu.CompilerParams(dimension_semantics=("arbitrary",),
                                             vmem_limit_bytes=V7X_VMEM_LIMIT_BYTES),
        name="in_proj",
    )(x2, nw, win, cos_t, sin_t, lnw, lnb, ws, sb, onw)


def _attn_kernel(q_ref, k_ref, v_ref, lam_ref, sw_ref, o_ref, m_ref, l_ref, acc_ref, *, tq, lam_init):
    qi = pl.program_id(1)
    lv = lam_ref[...]
    lam = (jnp.exp(jnp.sum(lv[0:1] * lv[1:2], axis=-1, keepdims=True))
           - jnp.exp(jnp.sum(lv[2:3] * lv[3:4], axis=-1, keepdims=True)) + lam_init)

    lane = lax.broadcasted_iota(jnp.int32, (tq, LANES), 1)
    low_half = lane < HEAD_DIM
    row = lax.broadcasted_iota(jnp.int32, (tq, tq), 0)
    col = lax.broadcasted_iota(jnp.int32, (tq, tq), 1)
    causal = col <= row
    n_var = 2 * SLAB_HEADS

    for c in range(N_SLABS):
        sl = slice(c * LANES, (c + 1) * LANES)
        qs = q_ref[:, sl]
        zero = jnp.zeros_like(qs)
        q_var = [jnp.where((lane >= x * DIFF_QK_DIM) & (lane < (x + 1) * DIFF_QK_DIM), qs, zero)
                 for x in range(n_var)]
        m_ref[...] = jnp.full(m_ref.shape, NEG_INF, F32)
        l_ref[...] = jnp.zeros(l_ref.shape, F32)
        acc_ref[...] = jnp.zeros(acc_ref.shape, F32)

        def step(j, diagonal):
            start = pl.multiple_of(j * tq, tq)
            kt = k_ref[pl.ds(start, tq), sl]
            vt = v_ref[pl.ds(start, tq), sl]
            for x in range(n_var):
                s = lax.dot_general(q_var[x], kt, (((1,), (1,)), ((), ())), preferred_element_type=F32)
                if diagonal:
                    s = jnp.where(causal, s, NEG_INF)
                m_prev = m_ref[x]
                m_next = jnp.maximum(m_prev, jnp.max(s, axis=1, keepdims=True))
                alpha = jnp.exp(m_prev - m_next)
                p = jnp.exp(s - m_next[:, 0:1])
                l_ref[x] = alpha * l_ref[x] + jnp.sum(p, axis=1, keepdims=True)
                acc_ref[x] = alpha * acc_ref[x] + jnp.dot(p.astype(BF16), vt, preferred_element_type=F32)
                m_ref[x] = m_next

        def body(j, carry):
            step(j, False)
            return carry

        lax.fori_loop(0, qi, body, 0)
        step(qi, True)

        o_lo = acc_ref[0] / l_ref[0] - lam * (acc_ref[1] / l_ref[1])
        o_hi = acc_ref[2] / l_ref[2] - lam * (acc_ref[3] / l_ref[3])
        o = jnp.where(low_half, o_lo, o_hi)
        o_ref[:, sl] = (_group_rms(o, sw_ref[...], low_half) * (1.0 - lam_init)).astype(BF16)


def _attention(q, k, v, lam_vecs, sw, *, batch, seq, tq, lam_init):
    T = q.shape[0]
    nq = seq // tq
    n_var = 2 * SLAB_HEADS
    return pl.pallas_call(
        functools.partial(_attn_kernel, tq=tq, lam_init=lam_init),
        grid=(batch, nq),
        in_specs=[
            pl.BlockSpec((tq, DIFF_WIDTH), lambda b, i: (b * nq + i, 0)),
            pl.BlockSpec((seq, DIFF_WIDTH), lambda b, i: (b, 0)),
            pl.BlockSpec((seq, DIFF_WIDTH), lambda b, i: (b, 0)),
            pl.BlockSpec((4, DIFF_QK_DIM), lambda b, i: (0, 0)),
            pl.BlockSpec((1, LANES), lambda b, i: (0, 0)),
        ],
        out_specs=pl.BlockSpec((tq, DIFF_WIDTH), lambda b, i: (b * nq + i, 0)),
        out_shape=jax.ShapeDtypeStruct((T, DIFF_WIDTH), BF16),
        scratch_shapes=[pltpu.VMEM((n_var, tq, LANES), F32)] * 3,
        compiler_params=pltpu.CompilerParams(dimension_semantics=("arbitrary", "arbitrary"),
                                             vmem_limit_bytes=V7X_VMEM_LIMIT_BYTES),
        name="diff_attn",
    )(q, k, v, lam_vecs, sw)


def _out_proj_kernel(x_ref, a_ref, g_ref, wa_ref, wg_ref, fw_ref, x1_ref, hn_ref):
    x1 = (x_ref[...] + jnp.dot(a_ref[...], wa_ref[...], preferred_element_type=F32)
          + jnp.dot(g_ref[...], wg_ref[...], preferred_element_type=F32))
    x1_ref[...] = x1
    hn_ref[...] = _rms(x1, fw_ref[...]).astype(BF16)


def _out_proj(x2, attn, sgu, wa, wg, fw, *, tm):
    T = x2.shape[0]
    full = lambda shape: pl.BlockSpec(shape, lambda i: (0,) * len(shape))
    return pl.pallas_call(
        _out_proj_kernel,
        grid=(T // tm,),
        in_specs=[
            pl.BlockSpec((tm, D_MODEL), lambda i: (i, 0)),
            pl.BlockSpec((tm, DIFF_WIDTH), lambda i: (i, 0)),
            pl.BlockSpec((tm, SGU_WIDTH), lambda i: (i, 0)),
            full((DIFF_WIDTH, D_MODEL)),
            full((SGU_WIDTH, D_MODEL)),
            full((1, D_MODEL)),
        ],
        out_specs=[pl.BlockSpec((tm, D_MODEL), lambda i: (i, 0))] * 2,
        out_shape=[jax.ShapeDtypeStruct((T, D_MODEL), F32), jax.ShapeDtypeStruct((T, D_MODEL), BF16)],
        compiler_params=pltpu.CompilerParams(dimension_semantics=("arbitrary",),
                                             vmem_limit_bytes=V7X_VMEM_LIMIT_BYTES),
        name="out_proj",
    )(x2, attn, sgu, wa, wg, fw)


def _top16(s, key_idx):
    cur = s
    rank = jnp.full(s.shape, NOT_SELECTED, F32)
    vals = jnp.zeros((PEER_TOPK, s.shape[1]), F32)
    slot = lax.broadcasted_iota(jnp.int32, vals.shape, 0)
    for it in range(PEER_TOPK):
        mx = jnp.max(cur, axis=0, keepdims=True)
        first = jnp.min(jnp.where(cur == mx, key_idx, float(PEER_N_KEYS)), axis=0, keepdims=True)
        sel = key_idx == first
        cur = jnp.where(sel, NEG_INF, cur)
        rank = jnp.where(sel, float(it), rank)
        vals = jnp.where(slot == it, mx, vals)
    return rank, vals


_CAND_ROWS = 16 + 7 * 8 + 8


def _cand_flat_index():
    r = lax.broadcasted_iota(jnp.int32, (_CAND_ROWS, LANES), 0)
    a = jnp.where(r < 16, 0, jnp.where(r < 72, 1 + ((r - 16) >> 3), 8 + (r - 72)))
    b = jnp.where(r < 16, r, jnp.where(r < 72, (r - 16) & 7, 0))
    return (a * PEER_TOPK + b).astype(F32)


def _select_pairs(v1, v2, flat):
    pieces = [v1[0:1] + v2]
    pieces += [v1[a:a + 1] + v2[0:8] for a in range(1, 8)]
    pieces += [v1[8:16] + v2[0:1]]
    cand = jnp.concatenate(pieces, axis=0)
    z = jnp.zeros((1, cand.shape[1]), F32)
    top = None
    for it in range(PEER_TOPK):
        mx = jnp.max(cand, axis=0, keepdims=True)
        if it == 0:
            top = mx
        first = jnp.min(jnp.where(cand == mx, flat, 1e9), axis=0, keepdims=True)
        cand = jnp.where(flat == first, NEG_INF, cand)
        z = z + jnp.exp(mx - top)
    picked = jnp.where(cand == NEG_INF, 1.0, 0.0)
    counts = [jnp.sum(picked[0:16], axis=0, keepdims=True)]
    counts += [jnp.sum(picked[16 + 8 * (a - 1):24 + 8 * (a - 1)], axis=0, keepdims=True) for a in range(1, 8)]
    counts += [picked[72 + a:73 + a] for a in range(8)]
    return counts, z


def _peer_kernel(hn_ref, x1_ref, wq_ref, keys_ref, u_ref, vt_ref, fw_ref, o_ref,
                 q_ref, l_ref, nb_ref, r_ref, rk_ref, a_ref, h_ref, y_ref, *, tm, te):
    e = pl.program_id(1)
    n_chunks = tm // LANES
    i_per_tile = te // PEER_N_KEYS
    nt = (((1,), (1,)), ((), ()))

    @pl.when(e == 0)
    def _select():
        y_ref[...] = jnp.zeros(y_ref.shape, F32)
        q = jnp.dot(hn_ref[...], wq_ref[...], preferred_element_type=F32).astype(BF16)
        for hh in range(PEER_HEADS):
            q_ref[hh] = q[:, hh * PEER_QUERY_DIM:(hh + 1) * PEER_QUERY_DIM]
        key_idx = lax.broadcasted_iota(jnp.int32, (PEER_N_KEYS, LANES), 0).astype(F32)
        flat = _cand_flat_index()

        def select(idx, carry):
            hh = idx // n_chunks
            lc = idx % n_chunks
            rows = pl.ds(pl.multiple_of(lc * LANES, LANES), LANES)
            s1 = lax.dot_general(keys_ref[0], q_ref[hh, rows, 0:PEER_HALF], nt,
                                 preferred_element_type=F32)
            s2 = lax.dot_general(keys_ref[1], q_ref[hh, rows, PEER_HALF:PEER_QUERY_DIM], nt,
                                 preferred_element_type=F32)
            rank1, v1 = _top16(s1, key_idx)
            rank2, v2 = _top16(s2, key_idx)
            counts, z = _select_pairs(v1, v2, flat)
            nb = jnp.zeros(rank1.shape, F32)
            for a in range(PEER_TOPK):
                nb = jnp.where(rank1 == float(a), counts[a], nb)
            nb_ref[hh, lc] = nb
            rk_ref[hh, lc] = rank2
            l_ref[hh, lc] = jnp.where(rank1 < float(PEER_TOPK), jnp.exp(s1 - v1[0:1]), 0.0)
            r_ref[hh, lc] = jnp.where(rank2 < float(PEER_TOPK), jnp.exp(s2 - v2[0:1]), 0.0) / z
            return carry

        lax.fori_loop(0, PEER_HEADS * n_chunks, select, 0)

    act_in = lax.dot_general(u_ref[...], hn_ref[...], nt, preferred_element_type=F32)
    for lc in range(n_chunks):
        a_ref[lc] = act_in[:, lc * LANES:(lc + 1) * LANES]
    i0 = pl.multiple_of(e * i_per_tile, i_per_tile)

    def weigh(lc, carry):
        for ii in range(i_per_tile):
            w = jnp.zeros((PEER_N_KEYS, LANES), F32)
            for hh in range(PEER_HEADS):
                nb_rows = nb_ref[hh, lc, pl.ds(i0, i_per_tile), :]
                l_rows = l_ref[hh, lc, pl.ds(i0, i_per_tile), :]
                w = w + (jnp.where(rk_ref[hh, lc] < nb_rows[ii:ii + 1], r_ref[hh, lc], 0.0)
                         * l_rows[ii:ii + 1])
            rows = slice(ii * PEER_N_KEYS, (ii + 1) * PEER_N_KEYS)
            h_ref[lc, rows, :] = (_gelu(a_ref[lc, rows, :]) * w).astype(BF16)
        return carry

    lax.fori_loop(0, n_chunks, weigh, 0)

    h_all = jnp.concatenate([h_ref[lc] for lc in range(n_chunks)], axis=1)
    y_ref[...] += jnp.dot(vt_ref[...], h_all, preferred_element_type=F32)

    @pl.when(e == pl.num_programs(1) - 1)
    def _finish():
        x2 = x1_ref[...] + y_ref[...].T
        o_ref[...] = _rms(x2, fw_ref[...])


def _peer(hn, x1, wq, keys, u_tab, vt_tab, fw, *, tm, te):
    T = hn.shape[0]
    assert te % (8 * PEER_N_KEYS) == 0
    n_chunks = tm // LANES
    full = lambda shape: pl.BlockSpec(shape, lambda i, e: (0,) * len(shape))
    sel_scratch = pltpu.VMEM((PEER_HEADS, n_chunks, PEER_N_KEYS, LANES), F32)
    return pl.pallas_call(
        functools.partial(_peer_kernel, tm=tm, te=te),
        grid=(T // tm, PEER_N_EXPERTS // te),
        in_specs=[
            pl.BlockSpec((tm, D_MODEL), lambda i, e: (i, 0)),
            pl.BlockSpec((tm, D_MODEL), lambda i, e: (i, 0)),
            full((D_MODEL, PEER_HEADS * PEER_QUERY_DIM)),
            full((2, PEER_N_KEYS, PEER_HALF)),
            pl.BlockSpec((te, D_MODEL), lambda i, e: (e, 0)),
            pl.BlockSpec((D_MODEL, te), lambda i, e: (0, e)),
            full((1, D_MODEL)),
        ],
        out_specs=pl.BlockSpec((tm, D_MODEL), lambda i, e: (i, 0)),
        out_shape=jax.ShapeDtypeStruct((T, D_MODEL), F32),
        scratch_shapes=[pltpu.VMEM((PEER_HEADS, tm, PEER_QUERY_DIM), BF16),
                        sel_scratch, sel_scratch, sel_scratch, sel_scratch,
                        pltpu.VMEM((n_chunks, te, LANES), F32), pltpu.VMEM((n_chunks, te, LANES), BF16),
                        pltpu.VMEM((D_MODEL, tm), F32)],
        compiler_params=pltpu.CompilerParams(dimension_semantics=("arbitrary", "arbitrary"),
                                             vmem_limit_bytes=V7X_VMEM_LIMIT_BYTES),
        name="peer",
    )(hn, x1, wq, keys, u_tab, vt_tab, fw)


def _rope_tables(seq):
    pos = jnp.arange(seq, dtype=F32)
    inv_freq = ROPE_THETA ** (-jnp.arange(0, DIFF_QK_DIM, 2, dtype=F32) / DIFF_QK_DIM)
    ang = pos[:, None] * inv_freq[None, :]
    cos = jnp.cos(ang)
    sin = jnp.sin(ang)
    reps = LANES // DIFF_QK_DIM
    cos_t = jnp.tile(jnp.concatenate([cos, cos], axis=-1), (1, reps))
    sin_t = jnp.tile(jnp.concatenate([-sin, sin], axis=-1), (1, reps))
    return cos_t, sin_t


def _tiles(batch, seq):
    tm = min(256, seq)
    tq = min(256, seq)
    t_peer = min(512, batch * seq)
    return tm, tq, t_peer, 1024


def kernel(x, attn_norm_w, w_in, lambda_q1, lambda_k1, lambda_q2, lambda_k2, subln_w, sgu_ln_w, sgu_ln_b,
           sgu_ws, sgu_b, sgu_out_norm_w, w_out, ffn_norm_w, peer_wq, peer_keys, peer_u, peer_v, final_norm_w):
    batch, seq, d_model = x.shape
    assert d_model == D_MODEL and seq % SGU_CHUNK == 0
    assert w_in.shape[0] == 1, "single-layer block: the PEER call folds in the closing RMSNorm"
    tm, tq, t_peer, te = _tiles(batch, seq)
    assert seq % tm == 0 and seq % tq == 0 and (batch * seq) % t_peer == 0
    cos_t, sin_t = _rope_tables(seq)
    x2 = x.reshape(batch * seq, D_MODEL)
    row = lambda w: w.reshape(1, -1)
    l = 0
    lam_init = 0.8 - 0.6 * math.exp(-0.3 * l)
    sb = jnp.repeat(sgu_b[l].T, HEAD_DIM, axis=1)
    q, k, v, sgu = _in_proj(x2, row(attn_norm_w[l]), w_in[l].astype(BF16), cos_t, sin_t,
                            row(sgu_ln_w[l]), row(sgu_ln_b[l]), sgu_ws[l], sb, row(sgu_out_norm_w[l]),
                            seq=seq, tm=tm)
    lam_vecs = jnp.stack([lambda_q1[l], lambda_k1[l], lambda_q2[l], lambda_k2[l]])
    attn = _attention(q, k, v, lam_vecs, row(jnp.tile(subln_w[l], SLAB_HEADS)),
                      batch=batch, seq=seq, tq=tq, lam_init=lam_init)
    wo = w_out[l].astype(BF16)
    x1, hn = _out_proj(x2, attn, sgu, wo[:DIFF_WIDTH], wo[DIFF_WIDTH:], row(ffn_norm_w[l]), tm=tm)
    out = _peer(hn, x1, peer_wq[l].astype(BF16), peer_keys[l].astype(BF16), peer_u[l].astype(BF16),
                peer_v[l].astype(BF16).T, row(final_norm_w), tm=t_peer, te=te)
    return out.reshape(batch, seq, D_MODEL)
```

```python
import functools
import math

import jax
import jax.numpy as jnp
from jax import lax
from jax.experimental import pallas as pl
from jax.experimental.pallas import tpu as pltpu

F32 = jnp.float32
BF16 = jnp.bfloat16

D_MODEL = 1024
HEAD_DIM = 64
N_DIFF_HEADS = 8
DIFF_QK_DIM = HEAD_DIM // 2
DIFF_WIDTH = N_DIFF_HEADS * HEAD_DIM
N_SGU_HEADS = 8
SGU_WIDTH = N_SGU_HEADS * HEAD_DIM
IN_COLS = 3 * DIFF_WIDTH + 2 * SGU_WIDTH
SGU_CHUNK = 128
ROPE_THETA = 10000.0
PEER_HEADS = 8
PEER_N_KEYS = 128
PEER_N_EXPERTS = PEER_N_KEYS * PEER_N_KEYS
PEER_TOPK = 16
PEER_QUERY_DIM = 256
PEER_HALF = PEER_QUERY_DIM // 2
NORM_EPS = 1e-6
LN_EPS = 1e-5

LANES = 128
SLAB_HEADS = LANES // HEAD_DIM
N_SLABS = DIFF_WIDTH // LANES
V7X_VMEM_LIMIT_BYTES = 56 * 1024 * 1024
NOT_SELECTED = 99.0
PEER_SUB = 256
PEER_TOK = 256
BF16_TILE_ROWS = 16
NEG_INF = float("-inf")
SQRT_HALF = math.sqrt(0.5)


def _gelu(z):
    return 0.5 * z * (1.0 + lax.erf(z * SQRT_HALF))


def _rms(x, w):
    return x * lax.rsqrt(jnp.mean(x * x, axis=-1, keepdims=True) + NORM_EPS) * w


def _group_rms(o, w, low_half):
    ss = o * o
    s_lo = jnp.sum(jnp.where(low_half, ss, 0.0), axis=-1, keepdims=True)
    s_hi = jnp.sum(jnp.where(low_half, 0.0, ss), axis=-1, keepdims=True)
    ms = jnp.where(low_half, s_lo, s_hi) * (1.0 / HEAD_DIM)
    return o * lax.rsqrt(ms + NORM_EPS) * w


def _in_proj_kernel(x_ref, nw_ref, win_ref, cos_ref, sin_ref, lnw_ref, lnb_ref, ws_ref, sb_ref, onw_ref,
                    q_ref, k_ref, vt_ref, g_ref, *, tm):
    x = x_ref[...]
    h = _rms(x, nw_ref[...])
    proj = jnp.dot(h.astype(BF16), win_ref[...], preferred_element_type=F32)

    cos = cos_ref[...]
    sin = sin_ref[...]
    lane = lax.broadcasted_iota(jnp.int32, (tm, LANES), 1)
    first_half = (lane & (DIFF_QK_DIM // 2)) == 0

    def rope(t):
        partner = jnp.where(first_half, pltpu.roll(t, LANES - DIFF_QK_DIM // 2, 1),
                            pltpu.roll(t, DIFF_QK_DIM // 2, 1))
        return t * cos + partner * sin

    qk_scale = DIFF_QK_DIM ** -0.5 * math.log2(math.e)
    for c in range(N_SLABS):
        sl = slice(c * LANES, (c + 1) * LANES)
        q_ref[:, sl] = (rope(proj[:, sl]) * qk_scale).astype(BF16)
        k_ref[:, sl] = rope(proj[:, DIFF_WIDTH + c * LANES:DIFF_WIDTH + (c + 1) * LANES]).astype(BF16)
    vt_ref[0] = proj[:, 2 * DIFF_WIDTH:3 * DIFF_WIDTH].T.astype(BF16)

    u = _gelu(proj[:, 3 * DIFF_WIDTH:3 * DIFF_WIDTH + SGU_WIDTH])
    vg = _gelu(proj[:, 3 * DIFF_WIDTH + SGU_WIDTH:])
    mu = jnp.mean(vg, axis=-1, keepdims=True)
    xc = vg - mu
    vgn = (xc * lax.rsqrt(jnp.mean(xc * xc, axis=-1, keepdims=True) + LN_EPS) * lnw_ref[...]
           + lnb_ref[...]).astype(BF16)

    row = lax.broadcasted_iota(jnp.int32, (SGU_CHUNK, SGU_CHUNK), 0)
    col = lax.broadcasted_iota(jnp.int32, (SGU_CHUNK, SGU_CHUNK), 1)
    causal = row >= col
    w_heads = [jnp.where(causal, ws_ref[hh], 0.0).astype(BF16) for hh in range(N_SGU_HEADS)]
    low_half = lax.broadcasted_iota(jnp.int32, (SGU_CHUNK, LANES), 1) < HEAD_DIM

    for ch in range(tm // SGU_CHUNK):
        rows = slice(ch * SGU_CHUNK, (ch + 1) * SGU_CHUNK)
        for c in range(SGU_WIDTH // LANES):
            sl = slice(c * LANES, (c + 1) * LANES)
            vs = vgn[rows, sl]
            r_lo = jnp.dot(w_heads[SLAB_HEADS * c], vs, preferred_element_type=F32)
            r_hi = jnp.dot(w_heads[SLAB_HEADS * c + 1], vs, preferred_element_type=F32)
            mixed = jnp.where(low_half, r_lo, r_hi) + sb_ref[:, sl]
            o = u[rows, sl] * mixed
            g_ref[rows, sl] = _group_rms(o, onw_ref[:, sl], low_half).astype(BF16)


def _in_proj(x2, nw, win, cos_t, sin_t, lnw, lnb, ws, sb, onw, *, seq, tm):
    T = x2.shape[0]
    n_seq_tiles = seq // tm
    full = lambda shape: pl.BlockSpec(shape, lambda i: (0,) * len(shape))
    out = jax.ShapeDtypeStruct((T, DIFF_WIDTH), BF16)
    row_tile = pl.BlockSpec((tm, DIFF_WIDTH), lambda i: (i, 0))
    return pl.pallas_call(
        functools.partial(_in_proj_kernel, tm=tm),
        grid=(T // tm,),
        in_specs=[
            pl.BlockSpec((tm, D_MODEL), lambda i: (i, 0)),
            full((1, D_MODEL)),
            full((D_MODEL, IN_COLS)),
            pl.BlockSpec((tm, LANES), lambda i: (i % n_seq_tiles, 0)),
            pl.BlockSpec((tm, LANES), lambda i: (i % n_seq_tiles, 0)),
            full((1, SGU_WIDTH)),
            full((1, SGU_WIDTH)),
            full((N_SGU_HEADS, SGU_CHUNK, SGU_CHUNK)),
            full((SGU_CHUNK, SGU_WIDTH)),
            full((1, SGU_WIDTH)),
        ],
        out_specs=[row_tile, row_tile, pl.BlockSpec((1, DIFF_WIDTH, tm), lambda i: (i, 0, 0)), row_tile],
        out_shape=[out, out, jax.ShapeDtypeStruct((T // tm, DIFF_WIDTH, tm), BF16), out],
        compiler_params=pltpu.CompilerParams(dimension_semantics=("arbitrary",),
                                             vmem_limit_bytes=V7X_VMEM_LIMIT_BYTES),
        name="in_proj",
    )(x2, nw, win, cos_t, sin_t, lnw, lnb, ws, sb, onw)


def _attn_kernel(q_ref, k_ref, vt_ref, lam_ref, sw_ref, o_ref, m_ref, l_ref, acc_ref, sa_ref, sb_ref,
                 *, tq, lam_init):
    qi = pl.program_id(1)
    lv = lam_ref[...]
    lam = (jnp.exp(jnp.sum(lv[0:1] * lv[1:2], axis=-1, keepdims=True))
           - jnp.exp(jnp.sum(lv[2:3] * lv[3:4], axis=-1, keepdims=True)) + lam_init)

    lane = lax.broadcasted_iota(jnp.int32, (tq, LANES), 1)
    low_rows = lax.broadcasted_iota(jnp.int32, (LANES, tq), 0) < HEAD_DIM
    key_pos = lax.broadcasted_iota(jnp.int32, (tq, tq), 0)
    qry_pos = lax.broadcasted_iota(jnp.int32, (tq, tq), 1)
    causal = key_pos <= qry_pos
    n_var = 2 * SLAB_HEADS
    nt = (((1,), (1,)), ((), ()))

    for c in range(N_SLABS):
        sl = slice(c * LANES, (c + 1) * LANES)
        qs = q_ref[:, sl]
        zero = jnp.zeros_like(qs)
        q_var = [jnp.where((lane >= x * DIFF_QK_DIM) & (lane < (x + 1) * DIFF_QK_DIM), qs, zero)
                 for x in range(n_var)]
        m_ref[...] = jnp.full(m_ref.shape, NEG_INF, F32)
        l_ref[...] = jnp.zeros(l_ref.shape, F32)
        acc_ref[...] = jnp.zeros(acc_ref.shape, F32)

        def score_tile(j, s_ref):
            kt = k_ref[pl.ds(pl.multiple_of(j * tq, tq), tq), sl]
            for x in range(n_var):
                s_ref[x] = lax.dot_general(kt, q_var[x], nt, preferred_element_type=F32)

        def softmax_pv(j, s_ref, diagonal):
            vt = vt_ref[j, sl, :]
            probs, alphas = [], []
            for x in range(n_var):
                s = s_ref[x]
                if diagonal:
                    s = jnp.where(causal, s, NEG_INF)
                m_prev = m_ref[x]
                m_next = jnp.maximum(m_prev, jnp.max(s, axis=0, keepdims=True))
                alpha = jnp.exp2(m_prev - m_next)
                p = jnp.exp2(s - m_next)
                l_ref[x] = alpha * l_ref[x] + jnp.sum(p, axis=0, keepdims=True)
                m_ref[x] = m_next
                probs.append(p.astype(BF16))
                alphas.append(alpha)
            for x in range(n_var):
                acc_ref[x] = alphas[x] * acc_ref[x] + jnp.dot(vt, probs[x], preferred_element_type=F32)

        score_tile(0, sa_ref)

        def two_tiles(t, carry):
            j = 2 * t
            score_tile(j + 1, sb_ref)
            softmax_pv(j, sa_ref, False)
            score_tile(j + 2, sa_ref)
            softmax_pv(j + 1, sb_ref, False)
            return carry

        lax.fori_loop(0, qi // 2, two_tiles, 0)

        @pl.when(qi % 2 == 1)
        def _odd_tail():
            score_tile(qi, sb_ref)
            softmax_pv(qi - 1, sa_ref, False)
            softmax_pv(qi, sb_ref, True)

        @pl.when(qi % 2 == 0)
        def _even_tail():
            softmax_pv(qi, sa_ref, True)

        o_lo = acc_ref[0] / l_ref[0] - lam * (acc_ref[1] / l_ref[1])
        o_hi = acc_ref[2] / l_ref[2] - lam * (acc_ref[3] / l_ref[3])
        o = jnp.where(low_rows, o_lo, o_hi)
        ss = o * o
        ms = jnp.where(low_rows, jnp.sum(ss[:HEAD_DIM], axis=0, keepdims=True),
                       jnp.sum(ss[HEAD_DIM:], axis=0, keepdims=True)) * (1.0 / HEAD_DIM)
        o = o * lax.rsqrt(ms + NORM_EPS) * (sw_ref[...] * (1.0 - lam_init))
        o_ref[:, sl] = o.T.astype(BF16)


def _attention(q, k, vt, lam_vecs, sw, *, batch, seq, tq, lam_init):
    T = q.shape[0]
    nq = seq // tq
    n_var = 2 * SLAB_HEADS
    assert vt.shape == (T // tq, DIFF_WIDTH, tq)
    return pl.pallas_call(
        functools.partial(_attn_kernel, tq=tq, lam_init=lam_init),
        grid=(batch, nq),
        in_specs=[
            pl.BlockSpec((tq, DIFF_WIDTH), lambda b, i: (b * nq + i, 0)),
            pl.BlockSpec((seq, DIFF_WIDTH), lambda b, i: (b, 0)),
            pl.BlockSpec((nq, DIFF_WIDTH, tq), lambda b, i: (b, 0, 0)),
            pl.BlockSpec((4, DIFF_QK_DIM), lambda b, i: (0, 0)),
            pl.BlockSpec((LANES, 1), lambda b, i: (0, 0)),
        ],
        out_specs=pl.BlockSpec((tq, DIFF_WIDTH), lambda b, i: (b * nq + i, 0)),
        out_shape=jax.ShapeDtypeStruct((T, DIFF_WIDTH), BF16),
        scratch_shapes=[pltpu.VMEM((n_var, 1, tq), F32), pltpu.VMEM((n_var, 1, tq), F32),
                        pltpu.VMEM((n_var, LANES, tq), F32), pltpu.VMEM((n_var, tq, tq), F32),
                        pltpu.VMEM((n_var, tq, tq), F32)],
        compiler_params=pltpu.CompilerParams(dimension_semantics=("arbitrary", "arbitrary"),
                                             vmem_limit_bytes=V7X_VMEM_LIMIT_BYTES),
        name="diff_attn",
    )(q, k, vt, lam_vecs, sw)


def _out_proj_kernel(x_ref, a_ref, g_ref, wa_ref, wg_ref, fw_ref, x1_ref, hn_ref):
    x1 = (x_ref[...] + jnp.dot(a_ref[...], wa_ref[...], preferred_element_type=F32)
          + jnp.dot(g_ref[...], wg_ref[...], preferred_element_type=F32))
    x1_ref[...] = x1
    hn_ref[...] = _rms(x1, fw_ref[...]).astype(BF16)


def _out_proj(x2, attn, sgu, wa, wg, fw, *, tm):
    T = x2.shape[0]
    full = lambda shape: pl.BlockSpec(shape, lambda i: (0,) * len(shape))
    return pl.pallas_call(
        _out_proj_kernel,
        grid=(T // tm,),
        in_specs=[
            pl.BlockSpec((tm, D_MODEL), lambda i: (i, 0)),
            pl.BlockSpec((tm, DIFF_WIDTH), lambda i: (i, 0)),
            pl.BlockSpec((tm, SGU_WIDTH), lambda i: (i, 0)),
            full((DIFF_WIDTH, D_MODEL)),
            full((SGU_WIDTH, D_MODEL)),
            full((1, D_MODEL)),
        ],
        out_specs=[pl.BlockSpec((tm, D_MODEL), lambda i: (i, 0))] * 2,
        out_shape=[jax.ShapeDtypeStruct((T, D_MODEL), F32), jax.ShapeDtypeStruct((T, D_MODEL), BF16)],
        compiler_params=pltpu.CompilerParams(dimension_semantics=("arbitrary",),
                                             vmem_limit_bytes=V7X_VMEM_LIMIT_BYTES),
        name="out_proj",
    )(x2, attn, sgu, wa, wg, fw)


def _top16(s, key_idx):
    cur = s
    rank = jnp.full(s.shape, NOT_SELECTED, F32)
    vals = jnp.zeros((PEER_TOPK, s.shape[1]), F32)
    slot = lax.broadcasted_iota(jnp.int32, vals.shape, 0)
    for it in range(PEER_TOPK):
        mx = jnp.max(cur, axis=0, keepdims=True)
        first = jnp.min(jnp.where(cur == mx, key_idx, float(PEER_N_KEYS)), axis=0, keepdims=True)
        sel = key_idx == first
        cur = jnp.where(sel, NEG_INF, cur)
        rank = jnp.where(sel, float(it), rank)
        vals = jnp.where(slot == it, mx, vals)
    return rank, vals


_CAND_ROWS = 16 + 7 * 8 + 8


def _cand_flat_index():
    r = lax.broadcasted_iota(jnp.int32, (_CAND_ROWS, LANES), 0)
    a = jnp.where(r < 16, 0, jnp.where(r < 72, 1 + ((r - 16) >> 3), 8 + (r - 72)))
    b = jnp.where(r < 16, r, jnp.where(r < 72, (r - 16) & 7, 0))
    return (a * PEER_TOPK + b).astype(F32)


def _select_pairs(v1, v2, flat):
    pieces = [v1[0:1] + v2]
    pieces += [v1[a:a + 1] + v2[0:8] for a in range(1, 8)]
    pieces += [v1[8:16] + v2[0:1]]
    cand = jnp.concatenate(pieces, axis=0)
    z = jnp.zeros((1, cand.shape[1]), F32)
    top = None
    for it in range(PEER_TOPK):
        mx = jnp.max(cand, axis=0, keepdims=True)
        if it == 0:
            top = mx
        first = jnp.min(jnp.where(cand == mx, flat, 1e9), axis=0, keepdims=True)
        cand = jnp.where(flat == first, NEG_INF, cand)
        z = z + jnp.exp(mx - top)
    picked = jnp.where(cand == NEG_INF, 1.0, 0.0)
    counts = [jnp.sum(picked[0:16], axis=0, keepdims=True)]
    counts += [jnp.sum(picked[16 + 8 * (a - 1):24 + 8 * (a - 1)], axis=0, keepdims=True) for a in range(1, 8)]
    counts += [picked[72 + a:73 + a] for a in range(8)]
    return counts, z


def _rows_bf16(row):
    packed = jnp.broadcast_to(row, (BF16_TILE_ROWS, LANES)).astype(BF16)
    return jnp.concatenate([packed] * (PEER_N_KEYS // BF16_TILE_ROWS), axis=0)


def _peer_kernel(hn_ref, x1_ref, wq_ref, keys_ref, u_ref, vt_ref, fw_ref, o_ref,
                 q_ref, l_ref, nb_ref, r_ref, rk_ref, a_ref, h_ref, y_ref, *, tm, te):
    e = pl.program_id(1)
    n_chunks = tm // LANES
    i_per_tile = te // PEER_N_KEYS
    nt = (((1,), (1,)), ((), ()))

    @pl.when(e == 0)
    def _select():
        y_ref[...] = jnp.zeros(y_ref.shape, F32)
        q = jnp.dot(hn_ref[...], wq_ref[...], preferred_element_type=F32).astype(BF16)
        for hh in range(PEER_HEADS):
            q_ref[hh] = q[:, hh * PEER_QUERY_DIM:(hh + 1) * PEER_QUERY_DIM]
        key_idx = lax.broadcasted_iota(jnp.int32, (PEER_N_KEYS, LANES), 0).astype(F32)
        flat = _cand_flat_index()

        def select(idx, carry):
            hh = idx // n_chunks
            lc = idx % n_chunks
            rows = pl.ds(pl.multiple_of(lc * LANES, LANES), LANES)
            s1 = lax.dot_general(keys_ref[0], q_ref[hh, rows, 0:PEER_HALF], nt,
                                 preferred_element_type=F32)
            s2 = lax.dot_general(keys_ref[1], q_ref[hh, rows, PEER_HALF:PEER_QUERY_DIM], nt,
                                 preferred_element_type=F32)
            rank1, v1 = _top16(s1, key_idx)
            rank2, v2 = _top16(s2, key_idx)
            counts, z = _select_pairs(v1, v2, flat)
            nb = jnp.zeros(rank1.shape, F32)
            for a in range(PEER_TOPK):
                nb = jnp.where(rank1 == float(a), counts[a], nb)
            nb_ref[hh, lc] = nb
            rk_ref[hh, lc] = rank2.astype(BF16)
            l_ref[hh, lc] = jnp.where(rank1 < float(PEER_TOPK), jnp.exp(s1 - v1[0:1]), 0.0)
            r_ref[hh, lc] = (jnp.where(rank2 < float(PEER_TOPK), jnp.exp(s2 - v2[0:1]), 0.0) / z).astype(BF16)
            return carry

        lax.fori_loop(0, PEER_HEADS * n_chunks, select, 0)

    n_sub = te // PEER_SUB
    n_tok = tm // PEER_TOK
    chunks_per_tok = PEER_TOK // LANES
    i_per_sub = PEER_SUB // PEER_N_KEYS
    for th in range(n_tok):
        hn = hn_ref[th * PEER_TOK:(th + 1) * PEER_TOK, :]
        for s in range(n_sub):
            a_ref[th, s] = lax.dot_general(u_ref[s * PEER_SUB:(s + 1) * PEER_SUB, :], hn, nt,
                                           preferred_element_type=F32)
    i0 = pl.multiple_of(e * i_per_tile, i_per_tile)
    for th in range(n_tok):
        y_new = None
        for s in range(n_sub):
            for cl in range(chunks_per_tok):
                lc = th * chunks_per_tok + cl
                lanes = slice(cl * LANES, (cl + 1) * LANES)
                for il in range(i_per_sub):
                    ii = s * i_per_sub + il
                    w = None
                    for hh in range(PEER_HEADS):
                        nb_rows = nb_ref[hh, lc, pl.ds(i0, i_per_tile), :]
                        l_rows = l_ref[hh, lc, pl.ds(i0, i_per_tile), :]
                        term = (jnp.where(rk_ref[hh, lc] < _rows_bf16(nb_rows[ii:ii + 1]),
                                          r_ref[hh, lc], jnp.zeros((), BF16))
                                * _rows_bf16(l_rows[ii:ii + 1]))
                        w = term if w is None else w + term
                    rows = slice(il * PEER_N_KEYS, (il + 1) * PEER_N_KEYS)
                    h_ref[th, s, rows, lanes] = _gelu(a_ref[th, s, rows, lanes]).astype(BF16) * w
            part = jnp.dot(vt_ref[:, s * PEER_SUB:(s + 1) * PEER_SUB], h_ref[th, s],
                           preferred_element_type=F32)
            y_new = part if y_new is None else y_new + part
        y_ref[:, th * PEER_TOK:(th + 1) * PEER_TOK] += y_new

    @pl.when(e == pl.num_programs(1) - 1)
    def _finish():
        x2 = x1_ref[...] + y_ref[...].T
        o_ref[...] = _rms(x2, fw_ref[...])


def _peer(hn, x1, wq, keys, u_tab, vt_tab, fw, *, tm, te):
    T = hn.shape[0]
    assert te % (8 * PEER_N_KEYS) == 0
    n_chunks = tm // LANES
    full = lambda shape: pl.BlockSpec(shape, lambda i, e: (0,) * len(shape))
    sel_scratch = lambda dtype: pltpu.VMEM((PEER_HEADS, n_chunks, PEER_N_KEYS, LANES), dtype)
    return pl.pallas_call(
        functools.partial(_peer_kernel, tm=tm, te=te),
        grid=(T // tm, PEER_N_EXPERTS // te),
        in_specs=[
            pl.BlockSpec((tm, D_MODEL), lambda i, e: (i, 0)),
            pl.BlockSpec((tm, D_MODEL), lambda i, e: (i, 0)),
            full((D_MODEL, PEER_HEADS * PEER_QUERY_DIM)),
            full((2, PEER_N_KEYS, PEER_HALF)),
            pl.BlockSpec((te, D_MODEL), lambda i, e: (e, 0)),
            pl.BlockSpec((D_MODEL, te), lambda i, e: (0, e)),
            full((1, D_MODEL)),
        ],
        out_specs=pl.BlockSpec((tm, D_MODEL), lambda i, e: (i, 0)),
        out_shape=jax.ShapeDtypeStruct((T, D_MODEL), F32),
        scratch_shapes=[pltpu.VMEM((PEER_HEADS, tm, PEER_QUERY_DIM), BF16),
                        sel_scratch(F32), sel_scratch(F32), sel_scratch(BF16), sel_scratch(BF16),
                        pltpu.VMEM((tm // PEER_TOK, te // PEER_SUB, PEER_SUB, PEER_TOK), F32),
                        pltpu.VMEM((tm // PEER_TOK, te // PEER_SUB, PEER_SUB, PEER_TOK), BF16),
                        pltpu.VMEM((D_MODEL, tm), F32)],
        compiler_params=pltpu.CompilerParams(dimension_semantics=("arbitrary", "arbitrary"),
                                             vmem_limit_bytes=V7X_VMEM_LIMIT_BYTES),
        name="peer",
    )(hn, x1, wq, keys, u_tab, vt_tab, fw)


def _rope_tables(seq):
    pos = jnp.arange(seq, dtype=F32)
    inv_freq = ROPE_THETA ** (-jnp.arange(0, DIFF_QK_DIM, 2, dtype=F32) / DIFF_QK_DIM)
    ang = pos[:, None] * inv_freq[None, :]
    cos = jnp.cos(ang)
    sin = jnp.sin(ang)
    reps = LANES // DIFF_QK_DIM
    cos_t = jnp.tile(jnp.concatenate([cos, cos], axis=-1), (1, reps))
    sin_t = jnp.tile(jnp.concatenate([-sin, sin], axis=-1), (1, reps))
    return cos_t, sin_t


def _tiles(batch, seq):
    tm = min(256, seq)
    tq = min(256, seq)
    t_peer = min(512, batch * seq)
    return tm, tq, t_peer, 1024


def kernel(x, attn_norm_w, w_in, lambda_q1, lambda_k1, lambda_q2, lambda_k2, subln_w, sgu_ln_w, sgu_ln_b,
           sgu_ws, sgu_b, sgu_out_norm_w, w_out, ffn_norm_w, peer_wq, peer_keys, peer_u, peer_v, final_norm_w):
    batch, seq, d_model = x.shape
    assert d_model == D_MODEL and seq % SGU_CHUNK == 0
    assert w_in.shape[0] == 1, "single-layer block: the PEER call folds in the closing RMSNorm"
    tm, tq, t_peer, te = _tiles(batch, seq)
    assert seq % tm == 0 and tm == tq and (batch * seq) % t_peer == 0
    cos_t, sin_t = _rope_tables(seq)
    x2 = x.reshape(batch * seq, D_MODEL)
    row = lambda w: w.reshape(1, -1)
    l = 0
    lam_init = 0.8 - 0.6 * math.exp(-0.3 * l)
    sb = jnp.repeat(sgu_b[l].T, HEAD_DIM, axis=1)
    q, k, vt, sgu = _in_proj(x2, row(attn_norm_w[l]), w_in[l].astype(BF16), cos_t, sin_t,
                             row(sgu_ln_w[l]), row(sgu_ln_b[l]), sgu_ws[l], sb, row(sgu_out_norm_w[l]),
                             seq=seq, tm=tm)
    lam_vecs = jnp.stack([lambda_q1[l], lambda_k1[l], lambda_q2[l], lambda_k2[l]])
    attn = _attention(q, k, vt, lam_vecs, jnp.tile(subln_w[l], SLAB_HEADS).reshape(LANES, 1),
                      batch=batch, seq=seq, tq=tq, lam_init=lam_init)
    wo = w_out[l].astype(BF16)
    x1, hn = _out_proj(x2, attn, sgu, wo[:DIFF_WIDTH], wo[DIFF_WIDTH:], row(ffn_norm_w[l]), tm=tm)
    out = _peer(hn, x1, peer_wq[l].astype(BF16), peer_keys[l].astype(BF16), peer_u[l].astype(BF16),
                peer_v[l].astype(BF16).T, row(final_norm_w), tm=t_peer, te=te)
    return out.reshape(batch, seq, D_MODEL)
```

```python
import functools
import math

import jax
import jax.numpy as jnp
from jax import lax
from jax.experimental import pallas as pl
from jax.experimental.pallas import tpu as pltpu

F32 = jnp.float32
BF16 = jnp.bfloat16

D_MODEL = 1024
HEAD_DIM = 64
N_DIFF_HEADS = 8
DIFF_QK_DIM = HEAD_DIM // 2
DIFF_WIDTH = N_DIFF_HEADS * HEAD_DIM
N_SGU_HEADS = 8
SGU_WIDTH = N_SGU_HEADS * HEAD_DIM
IN_COLS = 3 * DIFF_WIDTH + 2 * SGU_WIDTH
SGU_CHUNK = 128
ROPE_THETA = 10000.0
PEER_HEADS = 8
PEER_N_KEYS = 128
PEER_N_EXPERTS = PEER_N_KEYS * PEER_N_KEYS
PEER_TOPK = 16
PEER_QUERY_DIM = 256
PEER_HALF = PEER_QUERY_DIM // 2
NORM_EPS = 1e-6
LN_EPS = 1e-5

LANES = 128
SLAB_HEADS = LANES // HEAD_DIM
N_SLABS = DIFF_WIDTH // LANES
V7X_VMEM_LIMIT_BYTES = 56 * 1024 * 1024
NOT_SELECTED = 99.0
PEER_SUB = 256
PEER_TOK = 256
BF16_TILE_ROWS = 16
SELECT_UNROLL = 2
NEG_INF = float("-inf")
SQRT_HALF = math.sqrt(0.5)


def _gelu(z):
    return 0.5 * z * (1.0 + lax.erf(z * SQRT_HALF))


def _rms(x, w):
    return x * lax.rsqrt(jnp.mean(x * x, axis=-1, keepdims=True) + NORM_EPS) * w


def _group_rms(o, w, low_half):
    ss = o * o
    s_lo = jnp.sum(jnp.where(low_half, ss, 0.0), axis=-1, keepdims=True)
    s_hi = jnp.sum(jnp.where(low_half, 0.0, ss), axis=-1, keepdims=True)
    ms = jnp.where(low_half, s_lo, s_hi) * (1.0 / HEAD_DIM)
    return o * lax.rsqrt(ms + NORM_EPS) * w


def _in_proj_kernel(x_ref, nw_ref, win_ref, cos_ref, sin_ref, lnw_ref, lnb_ref, ws_ref, sb_ref, onw_ref,
                    q_ref, k_ref, vt_ref, g_ref, *, tm):
    x = x_ref[...]
    h = _rms(x, nw_ref[...])
    proj = jnp.dot(h.astype(BF16), win_ref[...], preferred_element_type=F32)

    cos = cos_ref[...]
    sin = sin_ref[...]
    lane = lax.broadcasted_iota(jnp.int32, (tm, LANES), 1)
    first_half = (lane & (DIFF_QK_DIM // 2)) == 0

    def rope(t):
        partner = jnp.where(first_half, pltpu.roll(t, LANES - DIFF_QK_DIM // 2, 1),
                            pltpu.roll(t, DIFF_QK_DIM // 2, 1))
        return t * cos + partner * sin

    qk_scale = DIFF_QK_DIM ** -0.5 * math.log2(math.e)
    for c in range(N_SLABS):
        sl = slice(c * LANES, (c + 1) * LANES)
        q_ref[:, sl] = (rope(proj[:, sl]) * qk_scale).astype(BF16)
        k_ref[:, sl] = rope(proj[:, DIFF_WIDTH + c * LANES:DIFF_WIDTH + (c + 1) * LANES]).astype(BF16)
    vt_ref[0] = proj[:, 2 * DIFF_WIDTH:3 * DIFF_WIDTH].T.astype(BF16)

    u = _gelu(proj[:, 3 * DIFF_WIDTH:3 * DIFF_WIDTH + SGU_WIDTH])
    vg = _gelu(proj[:, 3 * DIFF_WIDTH + SGU_WIDTH:])
    mu = jnp.mean(vg, axis=-1, keepdims=True)
    xc = vg - mu
    vgn = (xc * lax.rsqrt(jnp.mean(xc * xc, axis=-1, keepdims=True) + LN_EPS) * lnw_ref[...]
           + lnb_ref[...]).astype(BF16)

    row = lax.broadcasted_iota(jnp.int32, (SGU_CHUNK, SGU_CHUNK), 0)
    col = lax.broadcasted_iota(jnp.int32, (SGU_CHUNK, SGU_CHUNK), 1)
    causal = row >= col
    w_heads = [jnp.where(causal, ws_ref[hh], 0.0).astype(BF16) for hh in range(N_SGU_HEADS)]
    low_half = lax.broadcasted_iota(jnp.int32, (SGU_CHUNK, LANES), 1) < HEAD_DIM

    for ch in range(tm // SGU_CHUNK):
        rows = slice(ch * SGU_CHUNK, (ch + 1) * SGU_CHUNK)
        for c in range(SGU_WIDTH // LANES):
            sl = slice(c * LANES, (c + 1) * LANES)
            vs = vgn[rows, sl]
            r_lo = jnp.dot(w_heads[SLAB_HEADS * c], vs, preferred_element_type=F32)
            r_hi = jnp.dot(w_heads[SLAB_HEADS * c + 1], vs, preferred_element_type=F32)
            mixed = jnp.where(low_half, r_lo, r_hi) + sb_ref[:, sl]
            o = u[rows, sl] * mixed
            g_ref[rows, sl] = _group_rms(o, onw_ref[:, sl], low_half).astype(BF16)


def _in_proj(x2, nw, win, cos_t, sin_t, lnw, lnb, ws, sb, onw, *, seq, tm):
    T = x2.shape[0]
    n_seq_tiles = seq // tm
    full = lambda shape: pl.BlockSpec(shape, lambda i: (0,) * len(shape))
    out = jax.ShapeDtypeStruct((T, DIFF_WIDTH), BF16)
    row_tile = pl.BlockSpec((tm, DIFF_WIDTH), lambda i: (i, 0))
    return pl.pallas_call(
        functools.partial(_in_proj_kernel, tm=tm),
        grid=(T // tm,),
        in_specs=[
            pl.BlockSpec((tm, D_MODEL), lambda i: (i, 0)),
            full((1, D_MODEL)),
            full((D_MODEL, IN_COLS)),
            pl.BlockSpec((tm, LANES), lambda i: (i % n_seq_tiles, 0)),
            pl.BlockSpec((tm, LANES), lambda i: (i % n_seq_tiles, 0)),
            full((1, SGU_WIDTH)),
            full((1, SGU_WIDTH)),
            full((N_SGU_HEADS, SGU_CHUNK, SGU_CHUNK)),
            full((SGU_CHUNK, SGU_WIDTH)),
            full((1, SGU_WIDTH)),
        ],
        out_specs=[row_tile, row_tile, pl.BlockSpec((1, DIFF_WIDTH, tm), lambda i: (i, 0, 0)), row_tile],
        out_shape=[out, out, jax.ShapeDtypeStruct((T // tm, DIFF_WIDTH, tm), BF16), out],
        compiler_params=pltpu.CompilerParams(dimension_semantics=("arbitrary",),
                                             vmem_limit_bytes=V7X_VMEM_LIMIT_BYTES),
        name="in_proj",
    )(x2, nw, win, cos_t, sin_t, lnw, lnb, ws, sb, onw)


def _attn_kernel(q_ref, k_ref, vt_ref, lam_ref, sw_ref, o_ref, m_ref, l_ref, acc_ref, sa_ref, sb_ref,
                 *, tq, lam_init):
    qi = pl.program_id(1)
    lv = lam_ref[...]
    lam = (jnp.exp(jnp.sum(lv[0:1] * lv[1:2], axis=-1, keepdims=True))
           - jnp.exp(jnp.sum(lv[2:3] * lv[3:4], axis=-1, keepdims=True)) + lam_init)

    lane = lax.broadcasted_iota(jnp.int32, (tq, LANES), 1)
    low_rows = lax.broadcasted_iota(jnp.int32, (LANES, tq), 0) < HEAD_DIM
    key_pos = lax.broadcasted_iota(jnp.int32, (tq, tq), 0)
    qry_pos = lax.broadcasted_iota(jnp.int32, (tq, tq), 1)
    causal = key_pos <= qry_pos
    n_var = 2 * SLAB_HEADS
    nt = (((1,), (1,)), ((), ()))

    for c in range(N_SLABS):
        sl = slice(c * LANES, (c + 1) * LANES)
        qs = q_ref[:, sl]
        zero = jnp.zeros_like(qs)
        q_var = [jnp.where((lane >= x * DIFF_QK_DIM) & (lane < (x + 1) * DIFF_QK_DIM), qs, zero)
                 for x in range(n_var)]
        m_ref[...] = jnp.full(m_ref.shape, NEG_INF, F32)
        l_ref[...] = jnp.zeros(l_ref.shape, F32)
        acc_ref[...] = jnp.zeros(acc_ref.shape, F32)

        def score_tile(j, s_ref):
            kt = k_ref[pl.ds(pl.multiple_of(j * tq, tq), tq), sl]
            for x in range(n_var):
                s_ref[x] = lax.dot_general(kt, q_var[x], nt, preferred_element_type=F32)

        def softmax_pv(j, s_ref, diagonal):
            vt = vt_ref[j, sl, :]
            probs, alphas = [], []
            for x in range(n_var):
                s = s_ref[x]
                if diagonal:
                    s = jnp.where(causal, s, NEG_INF)
                m_prev = m_ref[x]
                m_next = jnp.maximum(m_prev, jnp.max(s, axis=0, keepdims=True))
                alpha = jnp.exp2(m_prev - m_next)
                p = jnp.exp2(s - m_next)
                l_ref[x] = alpha * l_ref[x] + jnp.sum(p, axis=0, keepdims=True)
                m_ref[x] = m_next
                probs.append(p.astype(BF16))
                alphas.append(alpha)
            for x in range(n_var):
                acc_ref[x] = alphas[x] * acc_ref[x] + jnp.dot(vt, probs[x], preferred_element_type=F32)

        score_tile(0, sa_ref)

        def two_tiles(t, carry):
            j = 2 * t
            score_tile(j + 1, sb_ref)
            softmax_pv(j, sa_ref, False)
            score_tile(j + 2, sa_ref)
            softmax_pv(j + 1, sb_ref, False)
            return carry

        lax.fori_loop(0, qi // 2, two_tiles, 0)

        @pl.when(qi % 2 == 1)
        def _odd_tail():
            score_tile(qi, sb_ref)
            softmax_pv(qi - 1, sa_ref, False)
            softmax_pv(qi, sb_ref, True)

        @pl.when(qi % 2 == 0)
        def _even_tail():
            softmax_pv(qi, sa_ref, True)

        o_lo = acc_ref[0] / l_ref[0] - lam * (acc_ref[1] / l_ref[1])
        o_hi = acc_ref[2] / l_ref[2] - lam * (acc_ref[3] / l_ref[3])
        o = jnp.where(low_rows, o_lo, o_hi)
        ss = o * o
        ms = jnp.where(low_rows, jnp.sum(ss[:HEAD_DIM], axis=0, keepdims=True),
                       jnp.sum(ss[HEAD_DIM:], axis=0, keepdims=True)) * (1.0 / HEAD_DIM)
        o = o * lax.rsqrt(ms + NORM_EPS) * (sw_ref[...] * (1.0 - lam_init))
        o_ref[:, sl] = o.T.astype(BF16)


def _attention(q, k, vt, lam_vecs, sw, *, batch, seq, tq, lam_init):
    T = q.shape[0]
    nq = seq // tq
    n_var = 2 * SLAB_HEADS
    assert vt.shape == (T // tq, DIFF_WIDTH, tq)
    return pl.pallas_call(
        functools.partial(_attn_kernel, tq=tq, lam_init=lam_init),
        grid=(batch, nq),
        in_specs=[
            pl.BlockSpec((tq, DIFF_WIDTH), lambda b, i: (b * nq + i, 0)),
            pl.BlockSpec((seq, DIFF_WIDTH), lambda b, i: (b, 0)),
            pl.BlockSpec((nq, DIFF_WIDTH, tq), lambda b, i: (b, 0, 0)),
            pl.BlockSpec((4, DIFF_QK_DIM), lambda b, i: (0, 0)),
            pl.BlockSpec((LANES, 1), lambda b, i: (0, 0)),
        ],
        out_specs=pl.BlockSpec((tq, DIFF_WIDTH), lambda b, i: (b * nq + i, 0)),
        out_shape=jax.ShapeDtypeStruct((T, DIFF_WIDTH), BF16),
        scratch_shapes=[pltpu.VMEM((n_var, 1, tq), F32), pltpu.VMEM((n_var, 1, tq), F32),
                        pltpu.VMEM((n_var, LANES, tq), F32), pltpu.VMEM((n_var, tq, tq), F32),
                        pltpu.VMEM((n_var, tq, tq), F32)],
        compiler_params=pltpu.CompilerParams(dimension_semantics=("arbitrary", "arbitrary"),
                                             vmem_limit_bytes=V7X_VMEM_LIMIT_BYTES),
        name="diff_attn",
    )(q, k, vt, lam_vecs, sw)


def _out_proj_kernel(x_ref, a_ref, g_ref, wa_ref, wg_ref, fw_ref, x1_ref, hn_ref):
    x1 = (x_ref[...] + jnp.dot(a_ref[...], wa_ref[...], preferred_element_type=F32)
          + jnp.dot(g_ref[...], wg_ref[...], preferred_element_type=F32))
    x1_ref[...] = x1
    hn_ref[...] = _rms(x1, fw_ref[...]).astype(BF16)


def _out_proj(x2, attn, sgu, wa, wg, fw, *, tm):
    T = x2.shape[0]
    full = lambda shape: pl.BlockSpec(shape, lambda i: (0,) * len(shape))
    return pl.pallas_call(
        _out_proj_kernel,
        grid=(T // tm,),
        in_specs=[
            pl.BlockSpec((tm, D_MODEL), lambda i: (i, 0)),
            pl.BlockSpec((tm, DIFF_WIDTH), lambda i: (i, 0)),
            pl.BlockSpec((tm, SGU_WIDTH), lambda i: (i, 0)),
            full((DIFF_WIDTH, D_MODEL)),
            full((SGU_WIDTH, D_MODEL)),
            full((1, D_MODEL)),
        ],
        out_specs=[pl.BlockSpec((tm, D_MODEL), lambda i: (i, 0))] * 2,
        out_shape=[jax.ShapeDtypeStruct((T, D_MODEL), F32), jax.ShapeDtypeStruct((T, D_MODEL), BF16)],
        compiler_params=pltpu.CompilerParams(dimension_semantics=("arbitrary",),
                                             vmem_limit_bytes=V7X_VMEM_LIMIT_BYTES),
        name="out_proj",
    )(x2, attn, sgu, wa, wg, fw)


def _top16(s, key_idx, break_ties):
    cur = s
    rank = jnp.full(s.shape, NOT_SELECTED, F32)
    vals = jnp.zeros((PEER_TOPK, s.shape[1]), F32)
    slot = lax.broadcasted_iota(jnp.int32, vals.shape, 0)
    for it in range(PEER_TOPK):
        mx = jnp.max(cur, axis=0, keepdims=True)
        sel = cur == mx
        if break_ties:
            first = jnp.min(jnp.where(sel, key_idx, float(PEER_N_KEYS)), axis=0, keepdims=True)
            sel = key_idx == first
        cur = jnp.where(sel, NEG_INF, cur)
        rank = jnp.where(sel, float(it), rank)
        vals = jnp.where(slot == it, mx, vals)
    return rank, vals


_CAND_ROWS = 16 + 7 * 8 + 8


def _cand_flat_index():
    r = lax.broadcasted_iota(jnp.int32, (_CAND_ROWS, LANES), 0)
    a = jnp.where(r < 16, 0, jnp.where(r < 72, 1 + ((r - 16) >> 3), 8 + (r - 72)))
    b = jnp.where(r < 16, r, jnp.where(r < 72, (r - 16) & 7, 0))
    return (a * PEER_TOPK + b).astype(F32)


def _select_pairs(v1, v2, flat, break_ties):
    pieces = [v1[0:1] + v2]
    pieces += [v1[a:a + 1] + v2[0:8] for a in range(1, 8)]
    pieces += [v1[8:16] + v2[0:1]]
    cand = jnp.concatenate(pieces, axis=0)
    z = jnp.zeros((1, cand.shape[1]), F32)
    top = None
    for it in range(PEER_TOPK):
        mx = jnp.max(cand, axis=0, keepdims=True)
        if it == 0:
            top = mx
        sel = cand == mx
        if break_ties:
            first = jnp.min(jnp.where(sel, flat, 1e9), axis=0, keepdims=True)
            sel = flat == first
        cand = jnp.where(sel, NEG_INF, cand)
        z = z + jnp.exp(mx - top)
    picked = jnp.where(cand == NEG_INF, 1.0, 0.0)
    counts = [jnp.sum(picked[0:16], axis=0, keepdims=True)]
    counts += [jnp.sum(picked[16 + 8 * (a - 1):24 + 8 * (a - 1)], axis=0, keepdims=True) for a in range(1, 8)]
    counts += [picked[72 + a:73 + a] for a in range(8)]
    return counts, z


def _rows_bf16(row):
    packed = jnp.broadcast_to(row, (BF16_TILE_ROWS, LANES)).astype(BF16)
    return jnp.concatenate([packed] * (PEER_N_KEYS // BF16_TILE_ROWS), axis=0)


def _peer_kernel(hn_ref, x1_ref, wq_ref, keys_ref, u_ref, vt_ref, fw_ref, o_ref,
                 q_ref, l_ref, nb_ref, r_ref, rk_ref, a_ref, h_ref, y_ref, *, tm, te):
    e = pl.program_id(1)
    n_chunks = tm // LANES
    i_per_tile = te // PEER_N_KEYS
    nt = (((1,), (1,)), ((), ()))

    @pl.when(e == 0)
    def _select():
        y_ref[...] = jnp.zeros(y_ref.shape, F32)
        q = jnp.dot(hn_ref[...], wq_ref[...], preferred_element_type=F32).astype(BF16)
        for hh in range(PEER_HEADS):
            q_ref[hh] = q[:, hh * PEER_QUERY_DIM:(hh + 1) * PEER_QUERY_DIM]
        key_idx = lax.broadcasted_iota(jnp.int32, (PEER_N_KEYS, LANES), 0).astype(F32)
        flat = _cand_flat_index()

        def select_chunk(hh, lc, break_ties):
            rows = pl.ds(pl.multiple_of(lc * LANES, LANES), LANES)
            s1 = lax.dot_general(keys_ref[0], q_ref[hh, rows, 0:PEER_HALF], nt,
                                 preferred_element_type=F32)
            s2 = lax.dot_general(keys_ref[1], q_ref[hh, rows, PEER_HALF:PEER_QUERY_DIM], nt,
                                 preferred_element_type=F32)
            rank1, v1 = _top16(s1, key_idx, break_ties)
            rank2, v2 = _top16(s2, key_idx, break_ties)
            counts, z = _select_pairs(v1, v2, flat, break_ties)
            in1 = rank1 < float(PEER_TOPK)
            in2 = rank2 < float(PEER_TOPK)
            nb = jnp.zeros(rank1.shape, F32)
            for a in range(PEER_TOPK):
                nb = jnp.where(rank1 == float(a), counts[a], nb)
            nb_ref[hh, lc] = nb
            rk_ref[hh, lc] = rank2.astype(BF16)
            l_ref[hh, lc] = jnp.where(in1, jnp.exp(s1 - v1[0:1]), 0.0)
            r_ref[hh, lc] = (jnp.where(in2, jnp.exp(s2 - v2[0:1]), 0.0) / z).astype(BF16)
            taken = [jnp.sum(jnp.where(in1, 1.0, 0.0), axis=0, keepdims=True),
                     jnp.sum(jnp.where(in2, 1.0, 0.0), axis=0, keepdims=True), sum(counts)]
            return sum(jnp.sum(jnp.where(t == float(PEER_TOPK), 0.0, 1.0)) for t in taken)

        def select(idx, carry):
            hh = idx // (n_chunks // SELECT_UNROLL)
            lc0 = (idx % (n_chunks // SELECT_UNROLL)) * SELECT_UNROLL
            tokens_with_ties = sum(select_chunk(hh, lc0 + k, False) for k in range(SELECT_UNROLL))

            @pl.when(tokens_with_ties > 0.0)
            def _redo_in_top_k_order():
                for k in range(SELECT_UNROLL):
                    select_chunk(hh, lc0 + k, True)

            return carry

        lax.fori_loop(0, PEER_HEADS * n_chunks // SELECT_UNROLL, select, 0)

    n_sub = te // PEER_SUB
    n_tok = tm // PEER_TOK
    chunks_per_tok = PEER_TOK // LANES
    i_per_sub = PEER_SUB // PEER_N_KEYS
    for th in range(n_tok):
        hn = hn_ref[th * PEER_TOK:(th + 1) * PEER_TOK, :]
        for s in range(n_sub):
            a_ref[th, s] = lax.dot_general(u_ref[s * PEER_SUB:(s + 1) * PEER_SUB, :], hn, nt,
                                           preferred_element_type=F32)
    i0 = pl.multiple_of(e * i_per_tile, i_per_tile)
    for th in range(n_tok):
        y_new = None
        for s in range(n_sub):
            for cl in range(chunks_per_tok):
                lc = th * chunks_per_tok + cl
                lanes = slice(cl * LANES, (cl + 1) * LANES)
                for il in range(i_per_sub):
                    ii = s * i_per_sub + il
                    w = None
                    for hh in range(PEER_HEADS):
                        nb_rows = nb_ref[hh, lc, pl.ds(i0, i_per_tile), :]
                        l_rows = l_ref[hh, lc, pl.ds(i0, i_per_tile), :]
                        term = (jnp.where(rk_ref[hh, lc] < _rows_bf16(nb_rows[ii:ii + 1]),
                                          r_ref[hh, lc], jnp.zeros((), BF16))
                                * _rows_bf16(l_rows[ii:ii + 1]))
                        w = term if w is None else w + term
                    rows = slice(il * PEER_N_KEYS, (il + 1) * PEER_N_KEYS)
                    h_ref[th, s, rows, lanes] = _gelu(a_ref[th, s, rows, lanes]).astype(BF16) * w
            part = jnp.dot(vt_ref[0, :, s * PEER_SUB:(s + 1) * PEER_SUB], h_ref[th, s],
                           preferred_element_type=F32)
            y_new = part if y_new is None else y_new + part
        y_ref[:, th * PEER_TOK:(th + 1) * PEER_TOK] += y_new

    @pl.when(e == pl.num_programs(1) - 1)
    def _finish():
        x2 = x1_ref[...] + y_ref[...].T
        o_ref[...] = _rms(x2, fw_ref[...])


def _peer(hn, x1, wq, keys, u_tab, vt_tab, fw, *, tm, te):
    T = hn.shape[0]
    assert te % (8 * PEER_N_KEYS) == 0
    n_chunks = tm // LANES
    full = lambda shape: pl.BlockSpec(shape, lambda i, e: (0,) * len(shape))
    sel_scratch = lambda dtype: pltpu.VMEM((PEER_HEADS, n_chunks, PEER_N_KEYS, LANES), dtype)
    return pl.pallas_call(
        functools.partial(_peer_kernel, tm=tm, te=te),
        grid=(T // tm, PEER_N_EXPERTS // te),
        in_specs=[
            pl.BlockSpec((tm, D_MODEL), lambda i, e: (i, 0)),
            pl.BlockSpec((tm, D_MODEL), lambda i, e: (i, 0)),
            full((D_MODEL, PEER_HEADS * PEER_QUERY_DIM)),
            full((2, PEER_N_KEYS, PEER_HALF)),
            pl.BlockSpec((te, D_MODEL), lambda i, e: (e, 0)),
            pl.BlockSpec((1, D_MODEL, te), lambda i, e: (e, 0, 0)),
            full((1, D_MODEL)),
        ],
        out_specs=pl.BlockSpec((tm, D_MODEL), lambda i, e: (i, 0)),
        out_shape=jax.ShapeDtypeStruct((T, D_MODEL), F32),
        scratch_shapes=[pltpu.VMEM((PEER_HEADS, tm, PEER_QUERY_DIM), BF16),
                        sel_scratch(F32), sel_scratch(F32), sel_scratch(BF16), sel_scratch(BF16),
                        pltpu.VMEM((tm // PEER_TOK, te // PEER_SUB, PEER_SUB, PEER_TOK), F32),
                        pltpu.VMEM((tm // PEER_TOK, te // PEER_SUB, PEER_SUB, PEER_TOK), BF16),
                        pltpu.VMEM((D_MODEL, tm), F32)],
        compiler_params=pltpu.CompilerParams(dimension_semantics=("arbitrary", "arbitrary"),
                                             vmem_limit_bytes=V7X_VMEM_LIMIT_BYTES),
        name="peer",
    )(hn, x1, wq, keys, u_tab, vt_tab, fw)


def _rope_tables(seq):
    pos = jnp.arange(seq, dtype=F32)
    inv_freq = ROPE_THETA ** (-jnp.arange(0, DIFF_QK_DIM, 2, dtype=F32) / DIFF_QK_DIM)
    ang = pos[:, None] * inv_freq[None, :]
    cos = jnp.cos(ang)
    sin = jnp.sin(ang)
    reps = LANES // DIFF_QK_DIM
    cos_t = jnp.tile(jnp.concatenate([cos, cos], axis=-1), (1, reps))
    sin_t = jnp.tile(jnp.concatenate([-sin, sin], axis=-1), (1, reps))
    return cos_t, sin_t


def _tiles(batch, seq):
    tm = min(256, seq)
    tq = min(256, seq)
    t_peer = min(512, batch * seq)
    return tm, tq, t_peer, 1024


def kernel(x, attn_norm_w, w_in, lambda_q1, lambda_k1, lambda_q2, lambda_k2, subln_w, sgu_ln_w, sgu_ln_b,
           sgu_ws, sgu_b, sgu_out_norm_w, w_out, ffn_norm_w, peer_wq, peer_keys, peer_u, peer_v, final_norm_w):
    batch, seq, d_model = x.shape
    assert d_model == D_MODEL and seq % SGU_CHUNK == 0
    assert w_in.shape[0] == 1, "single-layer block: the PEER call folds in the closing RMSNorm"
    tm, tq, t_peer, te = _tiles(batch, seq)
    assert seq % tm == 0 and tm == tq and (batch * seq) % t_peer == 0
    cos_t, sin_t = _rope_tables(seq)
    x2 = x.reshape(batch * seq, D_MODEL)
    row = lambda w: w.reshape(1, -1)
    l = 0
    lam_init = 0.8 - 0.6 * math.exp(-0.3 * l)
    sb = jnp.repeat(sgu_b[l].T, HEAD_DIM, axis=1)
    q, k, vt, sgu = _in_proj(x2, row(attn_norm_w[l]), w_in[l].astype(BF16), cos_t, sin_t,
                             row(sgu_ln_w[l]), row(sgu_ln_b[l]), sgu_ws[l], sb, row(sgu_out_norm_w[l]),
                             seq=seq, tm=tm)
    lam_vecs = jnp.stack([lambda_q1[l], lambda_k1[l], lambda_q2[l], lambda_k2[l]])
    attn = _attention(q, k, vt, lam_vecs, jnp.tile(subln_w[l], SLAB_HEADS).reshape(LANES, 1),
                      batch=batch, seq=seq, tq=tq, lam_init=lam_init)
    wo = w_out[l].astype(BF16)
    x1, hn = _out_proj(x2, attn, sgu, wo[:DIFF_WIDTH], wo[DIFF_WIDTH:], row(ffn_norm_w[l]), tm=tm)
    vt_tiles = peer_v[l].astype(BF16).reshape(PEER_N_EXPERTS // te, te, D_MODEL).transpose(0, 2, 1)
    out = _peer(hn, x1, peer_wq[l].astype(BF16), peer_keys[l].astype(BF16), peer_u[l].astype(BF16),
                vt_tiles, row(final_norm_w), tm=t_peer, te=te)
    return out.reshape(batch, seq, D_MODEL)
```

```python
import functools
import math

import jax
import jax.numpy as jnp
from jax import lax
from jax.experimental import pallas as pl
from jax.experimental.pallas import tpu as pltpu

F32 = jnp.float32
BF16 = jnp.bfloat16

D_MODEL = 1024
HEAD_DIM = 64
N_DIFF_HEADS = 8
DIFF_QK_DIM = HEAD_DIM // 2
DIFF_WIDTH = N_DIFF_HEADS * HEAD_DIM
N_SGU_HEADS = 8
SGU_WIDTH = N_SGU_HEADS * HEAD_DIM
IN_COLS = 3 * DIFF_WIDTH + 2 * SGU_WIDTH
SGU_CHUNK = 128
ROPE_THETA = 10000.0
PEER_HEADS = 8
PEER_N_KEYS = 128
PEER_N_EXPERTS = PEER_N_KEYS * PEER_N_KEYS
PEER_TOPK = 16
PEER_QUERY_DIM = 256
PEER_HALF = PEER_QUERY_DIM // 2
NORM_EPS = 1e-6
LN_EPS = 1e-5

LANES = 128
SLAB_HEADS = LANES // HEAD_DIM
N_SLABS = DIFF_WIDTH // LANES
V7X_VMEM_LIMIT_BYTES = 56 * 1024 * 1024
NOT_SELECTED = 99.0
PEER_TILE = 1024
PEER_TOK = 256
SELECT_UNROLL = 2
NEG_INF = float("-inf")
SQRT_HALF = math.sqrt(0.5)


def _gelu(z):
    return 0.5 * z * (1.0 + lax.erf(z * SQRT_HALF))


def _rms(x, w):
    return x * lax.rsqrt(jnp.mean(x * x, axis=-1, keepdims=True) + NORM_EPS) * w


def _group_rms(o, w, low_half):
    ss = o * o
    s_lo = jnp.sum(jnp.where(low_half, ss, 0.0), axis=-1, keepdims=True)
    s_hi = jnp.sum(jnp.where(low_half, 0.0, ss), axis=-1, keepdims=True)
    ms = jnp.where(low_half, s_lo, s_hi) * (1.0 / HEAD_DIM)
    return o * lax.rsqrt(ms + NORM_EPS) * w


def _in_proj_kernel(x_ref, nw_ref, win_ref, cos_ref, sin_ref, lnw_ref, lnb_ref, ws_ref, sb_ref, onw_ref,
                    q_ref, k_ref, vt_ref, g_ref, *, tm):
    x = x_ref[...]
    h = _rms(x, nw_ref[...])
    proj = jnp.dot(h.astype(BF16), win_ref[...], preferred_element_type=F32)

    cos = cos_ref[...]
    sin = sin_ref[...]
    lane = lax.broadcasted_iota(jnp.int32, (tm, LANES), 1)
    first_half = (lane & (DIFF_QK_DIM // 2)) == 0

    def rope(t):
        partner = jnp.where(first_half, pltpu.roll(t, LANES - DIFF_QK_DIM // 2, 1),
                            pltpu.roll(t, DIFF_QK_DIM // 2, 1))
        return t * cos + partner * sin

    qk_scale = DIFF_QK_DIM ** -0.5 * math.log2(math.e)
    for c in range(N_SLABS):
        sl = slice(c * LANES, (c + 1) * LANES)
        q_ref[:, sl] = (rope(proj[:, sl]) * qk_scale).astype(BF16)
        k_ref[:, sl] = rope(proj[:, DIFF_WIDTH + c * LANES:DIFF_WIDTH + (c + 1) * LANES]).astype(BF16)
    vt_ref[0] = proj[:, 2 * DIFF_WIDTH:3 * DIFF_WIDTH].T.astype(BF16)

    u = _gelu(proj[:, 3 * DIFF_WIDTH:3 * DIFF_WIDTH + SGU_WIDTH])
    vg = _gelu(proj[:, 3 * DIFF_WIDTH + SGU_WIDTH:])
    mu = jnp.mean(vg, axis=-1, keepdims=True)
    xc = vg - mu
    vgn = (xc * lax.rsqrt(jnp.mean(xc * xc, axis=-1, keepdims=True) + LN_EPS) * lnw_ref[...]
           + lnb_ref[...]).astype(BF16)

    row = lax.broadcasted_iota(jnp.int32, (SGU_CHUNK, SGU_CHUNK), 0)
    col = lax.broadcasted_iota(jnp.int32, (SGU_CHUNK, SGU_CHUNK), 1)
    causal = row >= col
    w_heads = [jnp.where(causal, ws_ref[hh], 0.0).astype(BF16) for hh in range(N_SGU_HEADS)]
    low_half = lax.broadcasted_iota(jnp.int32, (SGU_CHUNK, LANES), 1) < HEAD_DIM

    for ch in range(tm // SGU_CHUNK):
        rows = slice(ch * SGU_CHUNK, (ch + 1) * SGU_CHUNK)
        for c in range(SGU_WIDTH // LANES):
            sl = slice(c * LANES, (c + 1) * LANES)
            vs = vgn[rows, sl]
            r_lo = jnp.dot(w_heads[SLAB_HEADS * c], vs, preferred_element_type=F32)
            r_hi = jnp.dot(w_heads[SLAB_HEADS * c + 1], vs, preferred_element_type=F32)
            mixed = jnp.where(low_half, r_lo, r_hi) + sb_ref[:, sl]
            o = u[rows, sl] * mixed
            g_ref[rows, sl] = _group_rms(o, onw_ref[:, sl], low_half).astype(BF16)


def _in_proj(x2, nw, win, cos_t, sin_t, lnw, lnb, ws, sb, onw, *, seq, tm):
    T = x2.shape[0]
    n_seq_tiles = seq // tm
    full = lambda shape: pl.BlockSpec(shape, lambda i: (0,) * len(shape))
    out = jax.ShapeDtypeStruct((T, DIFF_WIDTH), BF16)
    row_tile = pl.BlockSpec((tm, DIFF_WIDTH), lambda i: (i, 0))
    return pl.pallas_call(
        functools.partial(_in_proj_kernel, tm=tm),
        grid=(T // tm,),
        in_specs=[
            pl.BlockSpec((tm, D_MODEL), lambda i: (i, 0)),
            full((1, D_MODEL)),
            full((D_MODEL, IN_COLS)),
            pl.BlockSpec((tm, LANES), lambda i: (i % n_seq_tiles, 0)),
            pl.BlockSpec((tm, LANES), lambda i: (i % n_seq_tiles, 0)),
            full((1, SGU_WIDTH)),
            full((1, SGU_WIDTH)),
            full((N_SGU_HEADS, SGU_CHUNK, SGU_CHUNK)),
            full((SGU_CHUNK, SGU_WIDTH)),
            full((1, SGU_WIDTH)),
        ],
        out_specs=[row_tile, row_tile, pl.BlockSpec((1, DIFF_WIDTH, tm), lambda i: (i, 0, 0)), row_tile],
        out_shape=[out, out, jax.ShapeDtypeStruct((T // tm, DIFF_WIDTH, tm), BF16), out],
        compiler_params=pltpu.CompilerParams(dimension_semantics=("arbitrary",),
                                             vmem_limit_bytes=V7X_VMEM_LIMIT_BYTES),
        name="in_proj",
    )(x2, nw, win, cos_t, sin_t, lnw, lnb, ws, sb, onw)


def _attn_kernel(q_ref, k_ref, vt_ref, lam_ref, sw_ref, o_ref, m_ref, l_ref, acc_ref, sa_ref, sb_ref,
                 *, tq, lam_init):
    qi = pl.program_id(1)
    lv = lam_ref[...]
    lam = (jnp.exp(jnp.sum(lv[0:1] * lv[1:2], axis=-1, keepdims=True))
           - jnp.exp(jnp.sum(lv[2:3] * lv[3:4], axis=-1, keepdims=True)) + lam_init)

    lane = lax.broadcasted_iota(jnp.int32, (tq, LANES), 1)
    low_rows = lax.broadcasted_iota(jnp.int32, (LANES, tq), 0) < HEAD_DIM
    key_pos = lax.broadcasted_iota(jnp.int32, (tq, tq), 0)
    qry_pos = lax.broadcasted_iota(jnp.int32, (tq, tq), 1)
    causal = key_pos <= qry_pos
    n_var = 2 * SLAB_HEADS
    nt = (((1,), (1,)), ((), ()))

    for c in range(N_SLABS):
        sl = slice(c * LANES, (c + 1) * LANES)
        qs = q_ref[:, sl]
        zero = jnp.zeros_like(qs)
        q_var = [jnp.where((lane >= x * DIFF_QK_DIM) & (lane < (x + 1) * DIFF_QK_DIM), qs, zero)
                 for x in range(n_var)]
        m_ref[...] = jnp.full(m_ref.shape, NEG_INF, F32)
        l_ref[...] = jnp.zeros(l_ref.shape, F32)
        acc_ref[...] = jnp.zeros(acc_ref.shape, F32)

        def score_tile(j, s_ref):
            kt = k_ref[pl.ds(pl.multiple_of(j * tq, tq), tq), sl]
            for x in range(n_var):
                s_ref[x] = lax.dot_general(kt, q_var[x], nt, preferred_element_type=F32)

        def softmax_pv(j, s_ref, diagonal):
            vt = vt_ref[j, sl, :]
            probs, alphas = [], []
            for x in range(n_var):
                s = s_ref[x]
                if diagonal:
                    s = jnp.where(causal, s, NEG_INF)
                m_prev = m_ref[x]
                m_next = jnp.maximum(m_prev, jnp.max(s, axis=0, keepdims=True))
                alpha = jnp.exp2(m_prev - m_next)
                p = jnp.exp2(s - m_next)
                l_ref[x] = alpha * l_ref[x] + jnp.sum(p, axis=0, keepdims=True)
                m_ref[x] = m_next
                probs.append(p.astype(BF16))
                alphas.append(alpha)
            for x in range(n_var):
                acc_ref[x] = alphas[x] * acc_ref[x] + jnp.dot(vt, probs[x], preferred_element_type=F32)

        score_tile(0, sa_ref)

        def two_tiles(t, carry):
            j = 2 * t
            score_tile(j + 1, sb_ref)
            softmax_pv(j, sa_ref, False)
            score_tile(j + 2, sa_ref)
            softmax_pv(j + 1, sb_ref, False)
            return carry

        lax.fori_loop(0, qi // 2, two_tiles, 0)

        @pl.when(qi % 2 == 1)
        def _odd_tail():
            score_tile(qi, sb_ref)
            softmax_pv(qi - 1, sa_ref, False)
            softmax_pv(qi, sb_ref, True)

        @pl.when(qi % 2 == 0)
        def _even_tail():
            softmax_pv(qi, sa_ref, True)

        o_lo = acc_ref[0] / l_ref[0] - lam * (acc_ref[1] / l_ref[1])
        o_hi = acc_ref[2] / l_ref[2] - lam * (acc_ref[3] / l_ref[3])
        o = jnp.where(low_rows, o_lo, o_hi)
        ss = o * o
        ms = jnp.where(low_rows, jnp.sum(ss[:HEAD_DIM], axis=0, keepdims=True),
                       jnp.sum(ss[HEAD_DIM:], axis=0, keepdims=True)) * (1.0 / HEAD_DIM)
        o = o * lax.rsqrt(ms + NORM_EPS) * (sw_ref[...] * (1.0 - lam_init))
        o_ref[:, sl] = o.T.astype(BF16)


def _attention(q, k, vt, lam_vecs, sw, *, batch, seq, tq, lam_init):
    T = q.shape[0]
    nq = seq // tq
    n_var = 2 * SLAB_HEADS
    assert vt.shape == (T // tq, DIFF_WIDTH, tq)
    return pl.pallas_call(
        functools.partial(_attn_kernel, tq=tq, lam_init=lam_init),
        grid=(batch, nq),
        in_specs=[
            pl.BlockSpec((tq, DIFF_WIDTH), lambda b, i: (b * nq + i, 0)),
            pl.BlockSpec((seq, DIFF_WIDTH), lambda b, i: (b, 0)),
            pl.BlockSpec((nq, DIFF_WIDTH, tq), lambda b, i: (b, 0, 0)),
            pl.BlockSpec((4, DIFF_QK_DIM), lambda b, i: (0, 0)),
            pl.BlockSpec((LANES, 1), lambda b, i: (0, 0)),
        ],
        out_specs=pl.BlockSpec((tq, DIFF_WIDTH), lambda b, i: (b * nq + i, 0)),
        out_shape=jax.ShapeDtypeStruct((T, DIFF_WIDTH), BF16),
        scratch_shapes=[pltpu.VMEM((n_var, 1, tq), F32), pltpu.VMEM((n_var, 1, tq), F32),
                        pltpu.VMEM((n_var, LANES, tq), F32), pltpu.VMEM((n_var, tq, tq), F32),
                        pltpu.VMEM((n_var, tq, tq), F32)],
        compiler_params=pltpu.CompilerParams(dimension_semantics=("arbitrary", "arbitrary"),
                                             vmem_limit_bytes=V7X_VMEM_LIMIT_BYTES),
        name="diff_attn",
    )(q, k, vt, lam_vecs, sw)


def _top16(s, key_idx, break_ties):
    cur = s
    rank = jnp.full(s.shape, NOT_SELECTED, F32)
    vals = jnp.zeros((PEER_TOPK, s.shape[1]), F32)
    slot = lax.broadcasted_iota(jnp.int32, vals.shape, 0)
    for it in range(PEER_TOPK):
        mx = jnp.max(cur, axis=0, keepdims=True)
        sel = cur == mx
        if break_ties:
            first = jnp.min(jnp.where(sel, key_idx, float(PEER_N_KEYS)), axis=0, keepdims=True)
            sel = key_idx == first
        cur = jnp.where(sel, NEG_INF, cur)
        rank = jnp.where(sel, float(it), rank)
        vals = jnp.where(slot == it, mx, vals)
    return rank, vals


_CAND_ROWS = 16 + 7 * 8 + 8


def _cand_flat_index():
    r = lax.broadcasted_iota(jnp.int32, (_CAND_ROWS, LANES), 0)
    a = jnp.where(r < 16, 0, jnp.where(r < 72, 1 + ((r - 16) >> 3), 8 + (r - 72)))
    b = jnp.where(r < 16, r, jnp.where(r < 72, (r - 16) & 7, 0))
    return (a * PEER_TOPK + b).astype(F32)


def _select_pairs(v1, v2, flat, break_ties):
    pieces = [v1[0:1] + v2]
    pieces += [v1[a:a + 1] + v2[0:8] for a in range(1, 8)]
    pieces += [v1[8:16] + v2[0:1]]
    cand = jnp.concatenate(pieces, axis=0)
    z = jnp.zeros((1, cand.shape[1]), F32)
    top = None
    for it in range(PEER_TOPK):
        mx = jnp.max(cand, axis=0, keepdims=True)
        if it == 0:
            top = mx
        sel = cand == mx
        if break_ties:
            first = jnp.min(jnp.where(sel, flat, 1e9), axis=0, keepdims=True)
            sel = flat == first
        cand = jnp.where(sel, NEG_INF, cand)
        z = z + jnp.exp(mx - top)
    picked = jnp.where(cand == NEG_INF, 1.0, 0.0)
    counts = [jnp.sum(picked[0:16], axis=0, keepdims=True)]
    counts += [jnp.sum(picked[16 + 8 * (a - 1):24 + 8 * (a - 1)], axis=0, keepdims=True) for a in range(1, 8)]
    counts += [picked[72 + a:73 + a] for a in range(8)]
    return counts, z


def _peer_kernel(x_ref, at_ref, sg_ref, wa_ref, wg_ref, nw_ref, wq_ref, keys_ref, u_ref, vt_ref, fw_ref, o_ref,
                 x1_ref, hn_ref, q_ref, l_ref, nb_ref, r_ref, rk_ref, a_ref, h_ref, y_ref, *, tm, te):
    g = pl.program_id(1)
    n_chunks = tm // LANES
    n_tok = tm // PEER_TOK
    chunks_per_tok = PEER_TOK // LANES
    i_per_tile = te // PEER_N_KEYS
    nt = (((1,), (1,)), ((), ()))

    @pl.when(g == 0)
    def _select():
        y_ref[...] = jnp.zeros(y_ref.shape, F32)
        x1 = (x_ref[...] + jnp.dot(at_ref[...], wa_ref[...], preferred_element_type=F32)
              + jnp.dot(sg_ref[...], wg_ref[...], preferred_element_type=F32))
        x1_ref[...] = x1
        hn_ref[...] = _rms(x1, nw_ref[...]).astype(BF16)
        q = jnp.dot(hn_ref[...], wq_ref[...], preferred_element_type=F32).astype(BF16)
        for hh in range(PEER_HEADS):
            q_ref[hh] = q[:, hh * PEER_QUERY_DIM:(hh + 1) * PEER_QUERY_DIM]
        key_idx = lax.broadcasted_iota(jnp.int32, (PEER_N_KEYS, LANES), 0).astype(F32)
        flat = _cand_flat_index()

        def select_chunk(hh, lc, break_ties):
            rows = pl.ds(pl.multiple_of(lc * LANES, LANES), LANES)
            s1 = lax.dot_general(keys_ref[0], q_ref[hh, rows, 0:PEER_HALF], nt,
                                 preferred_element_type=F32)
            s2 = lax.dot_general(keys_ref[1], q_ref[hh, rows, PEER_HALF:PEER_QUERY_DIM], nt,
                                 preferred_element_type=F32)
            rank1, v1 = _top16(s1, key_idx, break_ties)
            rank2, v2 = _top16(s2, key_idx, break_ties)
            counts, z = _select_pairs(v1, v2, flat, break_ties)
            in1 = rank1 < float(PEER_TOPK)
            in2 = rank2 < float(PEER_TOPK)
            nb = jnp.zeros(rank1.shape, F32)
            for a in range(PEER_TOPK):
                nb = jnp.where(rank1 == float(a), counts[a], nb)
            nb_ref[hh, lc] = nb
            rk_ref[hh, lc] = rank2
            l_ref[hh, lc] = jnp.where(in1, 0.5 * jnp.exp(s1 - v1[0:1]), 0.0)
            r_ref[hh, lc] = jnp.where(in2, jnp.exp(s2 - v2[0:1]), 0.0) / z
            taken = [jnp.sum(jnp.where(in1, 1.0, 0.0), axis=0, keepdims=True),
                     jnp.sum(jnp.where(in2, 1.0, 0.0), axis=0, keepdims=True), sum(counts)]
            return sum(jnp.sum(jnp.where(t == float(PEER_TOPK), 0.0, 1.0)) for t in taken)

        def select(idx, carry):
            hh = idx // (n_chunks // SELECT_UNROLL)
            lc0 = (idx % (n_chunks // SELECT_UNROLL)) * SELECT_UNROLL
            tokens_with_ties = sum(select_chunk(hh, lc0 + k, False) for k in range(SELECT_UNROLL))

            @pl.when(tokens_with_ties > 0.0)
            def _redo_in_top_k_order():
                for k in range(SELECT_UNROLL):
                    select_chunk(hh, lc0 + k, True)

            return carry

        lax.fori_loop(0, PEER_HEADS * n_chunks // SELECT_UNROLL, select, 0)

    i0 = pl.multiple_of(g * i_per_tile, i_per_tile)

    @pl.when(g >= 0)
    def _activations():
        for th in range(n_tok):
            a_ref[th] = lax.dot_general(u_ref[...], hn_ref[th * PEER_TOK:(th + 1) * PEER_TOK, :], nt,
                                        preferred_element_type=F32)

    @pl.when(g >= -1)
    def _gate():
        for lc in range(n_chunks):
            th, cl = divmod(lc, chunks_per_tok)
            lanes = slice(cl * LANES, (cl + 1) * LANES)
            for ii in range(i_per_tile):
                w = None
                for hh in range(PEER_HEADS):
                    nb_rows = nb_ref[hh, lc, pl.ds(i0, i_per_tile), :]
                    l_rows = l_ref[hh, lc, pl.ds(i0, i_per_tile), :]
                    term = (jnp.where(rk_ref[hh, lc] < nb_rows[ii:ii + 1], r_ref[hh, lc], 0.0)
                            * l_rows[ii:ii + 1])
                    w = term if w is None else w + term
                rows = slice(ii * PEER_N_KEYS, (ii + 1) * PEER_N_KEYS)
                a = a_ref[th, rows, lanes]
                h_ref[th, rows, lanes] = (a * (1.0 + lax.erf(a * SQRT_HALF)) * w).astype(BF16)

    @pl.when(g >= -2)
    def _values():
        for th in range(n_tok):
            y_ref[:, th * PEER_TOK:(th + 1) * PEER_TOK] += jnp.dot(vt_ref[0], h_ref[th],
                                                                   preferred_element_type=F32)

    @pl.when(g == pl.num_programs(1) - 1)
    def _finish():
        x2 = x1_ref[...] + y_ref[...].T
        o_ref[...] = _rms(x2, fw_ref[...])


def _out_proj_peer(x2, attn, sgu, wa, wg, nw, wq, keys, u_tab, vt_tiles, fw, *, tm):
    T = x2.shape[0]
    te = PEER_TILE
    assert te == 8 * PEER_N_KEYS
    assert vt_tiles.shape == (PEER_N_EXPERTS // te, D_MODEL, te)
    n_chunks = tm // LANES
    n_tok = tm // PEER_TOK
    full = lambda shape: pl.BlockSpec(shape, lambda i, g: (0,) * len(shape))
    tok_tile = lambda width: pl.BlockSpec((tm, width), lambda i, g: (i, 0))
    sel_scratch = pltpu.VMEM((PEER_HEADS, n_chunks, PEER_N_KEYS, LANES), F32)
    return pl.pallas_call(
        functools.partial(_peer_kernel, tm=tm, te=te),
        grid=(T // tm, PEER_N_EXPERTS // te),
        in_specs=[
            tok_tile(D_MODEL), tok_tile(DIFF_WIDTH), tok_tile(SGU_WIDTH),
            full((DIFF_WIDTH, D_MODEL)),
            full((SGU_WIDTH, D_MODEL)),
            full((1, D_MODEL)),
            full((D_MODEL, PEER_HEADS * PEER_QUERY_DIM)),
            full((2, PEER_N_KEYS, PEER_HALF)),
            pl.BlockSpec((te, D_MODEL), lambda i, g: (g, 0)),
            pl.BlockSpec((1, D_MODEL, te), lambda i, g: (g, 0, 0)),
            full((1, D_MODEL)),
        ],
        out_specs=tok_tile(D_MODEL),
        out_shape=jax.ShapeDtypeStruct((T, D_MODEL), F32),
        scratch_shapes=[pltpu.VMEM((tm, D_MODEL), F32), pltpu.VMEM((tm, D_MODEL), BF16),
                        pltpu.VMEM((PEER_HEADS, tm, PEER_QUERY_DIM), BF16),
                        sel_scratch, sel_scratch, sel_scratch, sel_scratch,
                        pltpu.VMEM((n_tok, te, PEER_TOK), F32), pltpu.VMEM((n_tok, te, PEER_TOK), BF16),
                        pltpu.VMEM((D_MODEL, tm), F32)],
        compiler_params=pltpu.CompilerParams(dimension_semantics=("arbitrary", "arbitrary"),
                                             vmem_limit_bytes=V7X_VMEM_LIMIT_BYTES),
        name="out_proj_peer",
    )(x2, attn, sgu, wa, wg, nw, wq, keys, u_tab, vt_tiles, fw)


def _rope_tables(seq):
    pos = jnp.arange(seq, dtype=F32)
    inv_freq = ROPE_THETA ** (-jnp.arange(0, DIFF_QK_DIM, 2, dtype=F32) / DIFF_QK_DIM)
    ang = pos[:, None] * inv_freq[None, :]
    cos = jnp.cos(ang)
    sin = jnp.sin(ang)
    reps = LANES // DIFF_QK_DIM
    cos_t = jnp.tile(jnp.concatenate([cos, cos], axis=-1), (1, reps))
    sin_t = jnp.tile(jnp.concatenate([-sin, sin], axis=-1), (1, reps))
    return cos_t, sin_t


def _tiles(batch, seq):
    tm = min(256, seq)
    tq = min(256, seq)
    t_peer = min(512, batch * seq)
    return tm, tq, t_peer


def kernel(x, attn_norm_w, w_in, lambda_q1, lambda_k1, lambda_q2, lambda_k2, subln_w, sgu_ln_w, sgu_ln_b,
           sgu_ws, sgu_b, sgu_out_norm_w, w_out, ffn_norm_w, peer_wq, peer_keys, peer_u, peer_v, final_norm_w):
    batch, seq, d_model = x.shape
    assert d_model == D_MODEL and seq % SGU_CHUNK == 0
    assert w_in.shape[0] == 1, "single-layer block: the PEER call folds in the closing RMSNorm"
    tm, tq, t_peer = _tiles(batch, seq)
    assert seq % tm == 0 and tm == tq and (batch * seq) % t_peer == 0
    cos_t, sin_t = _rope_tables(seq)
    x2 = x.reshape(batch * seq, D_MODEL)
    row = lambda w: w.reshape(1, -1)
    l = 0
    lam_init = 0.8 - 0.6 * math.exp(-0.3 * l)
    sb = jnp.repeat(sgu_b[l].T, HEAD_DIM, axis=1)
    q, k, vt, sgu = _in_proj(x2, row(attn_norm_w[l]), w_in[l].astype(BF16), cos_t, sin_t,
                             row(sgu_ln_w[l]), row(sgu_ln_b[l]), sgu_ws[l], sb, row(sgu_out_norm_w[l]),
                             seq=seq, tm=tm)
    lam_vecs = jnp.stack([lambda_q1[l], lambda_k1[l], lambda_q2[l], lambda_k2[l]])
    attn = _attention(q, k, vt, lam_vecs, jnp.tile(subln_w[l], SLAB_HEADS).reshape(LANES, 1),
                      batch=batch, seq=seq, tq=tq, lam_init=lam_init)
    wo = w_out[l].astype(BF16)
    vt_tiles = peer_v[l].astype(BF16).reshape(-1, PEER_TILE, D_MODEL).transpose(0, 2, 1)
    out = _out_proj_peer(x2, attn, sgu, wo[:DIFF_WIDTH], wo[DIFF_WIDTH:], row(ffn_norm_w[l]),
                         peer_wq[l].astype(BF16), peer_keys[l].astype(BF16), peer_u[l].astype(BF16),
                         vt_tiles, row(final_norm_w), tm=t_peer)
    return out.reshape(batch, seq, D_MODEL)
```

```python
import functools
import math

import jax
import jax.numpy as jnp
from jax import lax
from jax.experimental import pallas as pl
from jax.experimental.pallas import tpu as pltpu

F32 = jnp.float32
BF16 = jnp.bfloat16

D_MODEL = 1024
HEAD_DIM = 64
N_DIFF_HEADS = 8
DIFF_QK_DIM = HEAD_DIM // 2
DIFF_WIDTH = N_DIFF_HEADS * HEAD_DIM
N_SGU_HEADS = 8
SGU_WIDTH = N_SGU_HEADS * HEAD_DIM
IN_COLS = 3 * DIFF_WIDTH + 2 * SGU_WIDTH
SGU_CHUNK = 128
ROPE_THETA = 10000.0
PEER_HEADS = 8
PEER_N_KEYS = 128
PEER_N_EXPERTS = PEER_N_KEYS * PEER_N_KEYS
PEER_TOPK = 16
PEER_QUERY_DIM = 256
PEER_HALF = PEER_QUERY_DIM // 2
NORM_EPS = 1e-6
LN_EPS = 1e-5

LANES = 128
SLAB_HEADS = LANES // HEAD_DIM
N_SLABS = DIFF_WIDTH // LANES
V7X_VMEM_LIMIT_BYTES = 60000 * 1024
NOT_SELECTED = 99.0
PEER_TILE = 2048
PEER_TOK = 256
SELECT_UNROLL = 4
NEG_INF = float("-inf")
SQRT_HALF = math.sqrt(0.5)


def _gelu(z):
    return 0.5 * z * (1.0 + lax.erf(z * SQRT_HALF))


def _rms(x, w):
    return x * lax.rsqrt(jnp.mean(x * x, axis=-1, keepdims=True) + NORM_EPS) * w


def _group_rms(o, w, low_half):
    ss = o * o
    s_lo = jnp.sum(jnp.where(low_half, ss, 0.0), axis=-1, keepdims=True)
    s_hi = jnp.sum(jnp.where(low_half, 0.0, ss), axis=-1, keepdims=True)
    ms = jnp.where(low_half, s_lo, s_hi) * (1.0 / HEAD_DIM)
    return o * lax.rsqrt(ms + NORM_EPS) * w


def _in_proj_kernel(x_ref, nw_ref, win_ref, cos_ref, sin_ref, lnw_ref, lnb_ref, ws_ref, sb_ref, onw_ref,
                    q_ref, k_ref, vt_ref, g_ref, *, tm):
    x = x_ref[...]
    h = _rms(x, nw_ref[...])
    proj = jnp.dot(h.astype(BF16), win_ref[...], preferred_element_type=F32)

    cos = cos_ref[...]
    sin = sin_ref[...]
    lane = lax.broadcasted_iota(jnp.int32, (tm, LANES), 1)
    first_half = (lane & (DIFF_QK_DIM // 2)) == 0

    def rope(t):
        partner = jnp.where(first_half, pltpu.roll(t, LANES - DIFF_QK_DIM // 2, 1),
                            pltpu.roll(t, DIFF_QK_DIM // 2, 1))
        return t * cos + partner * sin

    qk_scale = DIFF_QK_DIM ** -0.5 * math.log2(math.e)
    for c in range(N_SLABS):
        sl = slice(c * LANES, (c + 1) * LANES)
        q_ref[:, sl] = (rope(proj[:, sl]) * qk_scale).astype(BF16)
        k_ref[:, sl] = rope(proj[:, DIFF_WIDTH + c * LANES:DIFF_WIDTH + (c + 1) * LANES]).astype(BF16)
    vt_ref[0] = proj[:, 2 * DIFF_WIDTH:3 * DIFF_WIDTH].T.astype(BF16)

    u = _gelu(proj[:, 3 * DIFF_WIDTH:3 * DIFF_WIDTH + SGU_WIDTH])
    vg = _gelu(proj[:, 3 * DIFF_WIDTH + SGU_WIDTH:])
    mu = jnp.mean(vg, axis=-1, keepdims=True)
    xc = vg - mu
    vgn = (xc * lax.rsqrt(jnp.mean(xc * xc, axis=-1, keepdims=True) + LN_EPS) * lnw_ref[...]
           + lnb_ref[...]).astype(BF16)

    row = lax.broadcasted_iota(jnp.int32, (SGU_CHUNK, SGU_CHUNK), 0)
    col = lax.broadcasted_iota(jnp.int32, (SGU_CHUNK, SGU_CHUNK), 1)
    causal = row >= col
    w_heads = [jnp.where(causal, ws_ref[hh], 0.0).astype(BF16) for hh in range(N_SGU_HEADS)]
    low_half = lax.broadcasted_iota(jnp.int32, (SGU_CHUNK, LANES), 1) < HEAD_DIM

    for ch in range(tm // SGU_CHUNK):
        rows = slice(ch * SGU_CHUNK, (ch + 1) * SGU_CHUNK)
        for c in range(SGU_WIDTH // LANES):
            sl = slice(c * LANES, (c + 1) * LANES)
            vs = vgn[rows, sl]
            r_lo = jnp.dot(w_heads[SLAB_HEADS * c], vs, preferred_element_type=F32)
            r_hi = jnp.dot(w_heads[SLAB_HEADS * c + 1], vs, preferred_element_type=F32)
            mixed = jnp.where(low_half, r_lo, r_hi) + sb_ref[:, sl]
            o = u[rows, sl] * mixed
            g_ref[rows, sl] = _group_rms(o, onw_ref[:, sl], low_half).astype(BF16)


def _in_proj(x2, nw, win, cos_t, sin_t, lnw, lnb, ws, sb, onw, *, seq, tm):
    T = x2.shape[0]
    n_seq_tiles = seq // tm
    full = lambda shape: pl.BlockSpec(shape, lambda i: (0,) * len(shape))
    out = jax.ShapeDtypeStruct((T, DIFF_WIDTH), BF16)
    row_tile = pl.BlockSpec((tm, DIFF_WIDTH), lambda i: (i, 0))
    return pl.pallas_call(
        functools.partial(_in_proj_kernel, tm=tm),
        grid=(T // tm,),
        in_specs=[
            pl.BlockSpec((tm, D_MODEL), lambda i: (i, 0)),
            full((1, D_MODEL)),
            full((D_MODEL, IN_COLS)),
            pl.BlockSpec((tm, LANES), lambda i: (i % n_seq_tiles, 0)),
            pl.BlockSpec((tm, LANES), lambda i: (i % n_seq_tiles, 0)),
            full((1, SGU_WIDTH)),
            full((1, SGU_WIDTH)),
            full((N_SGU_HEADS, SGU_CHUNK, SGU_CHUNK)),
            full((SGU_CHUNK, SGU_WIDTH)),
            full((1, SGU_WIDTH)),
        ],
        out_specs=[row_tile, row_tile, pl.BlockSpec((1, DIFF_WIDTH, tm), lambda i: (i, 0, 0)), row_tile],
        out_shape=[out, out, jax.ShapeDtypeStruct((T // tm, DIFF_WIDTH, tm), BF16), out],
        compiler_params=pltpu.CompilerParams(dimension_semantics=("arbitrary",),
                                             vmem_limit_bytes=V7X_VMEM_LIMIT_BYTES),
        name="in_proj",
    )(x2, nw, win, cos_t, sin_t, lnw, lnb, ws, sb, onw)


def _attn_kernel(q_ref, k_ref, vt_ref, lam_ref, sw_ref, o_ref, m_ref, l_ref, acc_ref, sa_ref, sb_ref,
                 *, tq, lam_init):
    qi = pl.program_id(1)
    lv = lam_ref[...]
    lam = (jnp.exp(jnp.sum(lv[0:1] * lv[1:2], axis=-1, keepdims=True))
           - jnp.exp(jnp.sum(lv[2:3] * lv[3:4], axis=-1, keepdims=True)) + lam_init)

    lane = lax.broadcasted_iota(jnp.int32, (tq, LANES), 1)
    low_rows = lax.broadcasted_iota(jnp.int32, (LANES, tq), 0) < HEAD_DIM
    key_pos = lax.broadcasted_iota(jnp.int32, (tq, tq), 0)
    qry_pos = lax.broadcasted_iota(jnp.int32, (tq, tq), 1)
    causal = key_pos <= qry_pos
    n_var = 2 * SLAB_HEADS
    nt = (((1,), (1,)), ((), ()))

    for c in range(N_SLABS):
        sl = slice(c * LANES, (c + 1) * LANES)
        qs = q_ref[:, sl]
        zero = jnp.zeros_like(qs)
        q_var = [jnp.where((lane >= x * DIFF_QK_DIM) & (lane < (x + 1) * DIFF_QK_DIM), qs, zero)
                 for x in range(n_var)]
        m_ref[...] = jnp.full(m_ref.shape, NEG_INF, F32)
        l_ref[...] = jnp.zeros(l_ref.shape, F32)
        acc_ref[...] = jnp.zeros(acc_ref.shape, F32)

        def score_tile(j, s_ref):
            kt = k_ref[pl.ds(pl.multiple_of(j * tq, tq), tq), sl]
            for x in range(n_var):
                s_ref[x] = lax.dot_general(kt, q_var[x], nt, preferred_element_type=F32)

        def softmax_pv(j, s_ref, diagonal):
            vt = vt_ref[j, sl, :]
            probs, alphas = [], []
            for x in range(n_var):
                s = s_ref[x]
                if diagonal:
                    s = jnp.where(causal, s, NEG_INF)
                m_prev = m_ref[x]
                m_next = jnp.maximum(m_prev, jnp.max(s, axis=0, keepdims=True))
                alpha = jnp.exp2(m_prev - m_next)
                p = jnp.exp2(s - m_next)
                l_ref[x] = alpha * l_ref[x] + jnp.sum(p, axis=0, keepdims=True)
                m_ref[x] = m_next
                probs.append(p.astype(BF16))
                alphas.append(alpha)
            for x in range(n_var):
                acc_ref[x] = alphas[x] * acc_ref[x] + jnp.dot(vt, probs[x], preferred_element_type=F32)

        score_tile(0, sa_ref)

        def two_tiles(t, carry):
            j = 2 * t
            score_tile(j + 1, sb_ref)
            softmax_pv(j, sa_ref, False)
            score_tile(j + 2, sa_ref)
            softmax_pv(j + 1, sb_ref, False)
            return carry

        lax.fori_loop(0, qi // 2, two_tiles, 0)

        @pl.when(qi % 2 == 1)
        def _odd_tail():
            score_tile(qi, sb_ref)
            softmax_pv(qi - 1, sa_ref, False)
            softmax_pv(qi, sb_ref, True)

        @pl.when(qi % 2 == 0)
        def _even_tail():
            softmax_pv(qi, sa_ref, True)

        o_lo = acc_ref[0] / l_ref[0] - lam * (acc_ref[1] / l_ref[1])
        o_hi = acc_ref[2] / l_ref[2] - lam * (acc_ref[3] / l_ref[3])
        o = jnp.where(low_rows, o_lo, o_hi)
        ss = o * o
        ms = jnp.where(low_rows, jnp.sum(ss[:HEAD_DIM], axis=0, keepdims=True),
                       jnp.sum(ss[HEAD_DIM:], axis=0, keepdims=True)) * (1.0 / HEAD_DIM)
        o = o * lax.rsqrt(ms + NORM_EPS) * (sw_ref[...] * (1.0 - lam_init))
        o_ref[:, sl] = o.T.astype(BF16)


def _attention(q, k, vt, lam_vecs, sw, *, batch, seq, tq, lam_init):
    T = q.shape[0]
    nq = seq // tq
    n_var = 2 * SLAB_HEADS
    assert vt.shape == (T // tq, DIFF_WIDTH, tq)
    return pl.pallas_call(
        functools.partial(_attn_kernel, tq=tq, lam_init=lam_init),
        grid=(batch, nq),
        in_specs=[
            pl.BlockSpec((tq, DIFF_WIDTH), lambda b, i: (b * nq + i, 0)),
            pl.BlockSpec((seq, DIFF_WIDTH), lambda b, i: (b, 0)),
            pl.BlockSpec((nq, DIFF_WIDTH, tq), lambda b, i: (b, 0, 0)),
            pl.BlockSpec((4, DIFF_QK_DIM), lambda b, i: (0, 0)),
            pl.BlockSpec((LANES, 1), lambda b, i: (0, 0)),
        ],
        out_specs=pl.BlockSpec((tq, DIFF_WIDTH), lambda b, i: (b * nq + i, 0)),
        out_shape=jax.ShapeDtypeStruct((T, DIFF_WIDTH), BF16),
        scratch_shapes=[pltpu.VMEM((n_var, 1, tq), F32), pltpu.VMEM((n_var, 1, tq), F32),
                        pltpu.VMEM((n_var, LANES, tq), F32), pltpu.VMEM((n_var, tq, tq), F32),
                        pltpu.VMEM((n_var, tq, tq), F32)],
        compiler_params=pltpu.CompilerParams(dimension_semantics=("arbitrary", "arbitrary"),
                                             vmem_limit_bytes=V7X_VMEM_LIMIT_BYTES),
        name="diff_attn",
    )(q, k, vt, lam_vecs, sw)


def _top16(s, key_idx, break_ties):
    cur = s
    rank = jnp.full(s.shape, NOT_SELECTED, F32)
    vals = jnp.zeros((PEER_TOPK, s.shape[1]), F32)
    slot = lax.broadcasted_iota(jnp.int32, vals.shape, 0)
    for it in range(PEER_TOPK):
        mx = jnp.max(cur, axis=0, keepdims=True)
        sel = cur == mx
        if break_ties:
            first = jnp.min(jnp.where(sel, key_idx, float(PEER_N_KEYS)), axis=0, keepdims=True)
            sel = key_idx == first
        cur = jnp.where(sel, NEG_INF, cur)
        rank = jnp.where(sel, float(it), rank)
        vals = jnp.where(slot == it, mx, vals)
    return rank, vals, cur == NEG_INF


_CAND_ROWS = 16 + 7 * 8 + 8


def _cand_flat_index():
    r = lax.broadcasted_iota(jnp.int32, (_CAND_ROWS, LANES), 0)
    a = jnp.where(r < 16, 0, jnp.where(r < 72, 1 + ((r - 16) >> 3), 8 + (r - 72)))
    b = jnp.where(r < 16, r, jnp.where(r < 72, (r - 16) & 7, 0))
    return (a * PEER_TOPK + b).astype(F32)


def _select_pairs(v1, v2, flat, break_ties):
    pieces = [v1[0:1] + v2]
    pieces += [v1[a:a + 1] + v2[0:8] for a in range(1, 8)]
    pieces += [v1[8:16] + v2[0:1]]
    cand = jnp.concatenate(pieces, axis=0)
    z = jnp.zeros((1, cand.shape[1]), F32)
    top = None
    for it in range(PEER_TOPK):
        mx = jnp.max(cand, axis=0, keepdims=True)
        if it == 0:
            top = mx
        sel = cand == mx
        if break_ties:
            first = jnp.min(jnp.where(sel, flat, 1e9), axis=0, keepdims=True)
            sel = flat == first
        cand = jnp.where(sel, NEG_INF, cand)
        z = z + jnp.exp(mx - top)
    picked = jnp.where(cand == NEG_INF, 1.0, 0.0)
    counts = [jnp.sum(picked[0:16], axis=0, keepdims=True)]
    counts += [jnp.sum(picked[16 + 8 * (a - 1):24 + 8 * (a - 1)], axis=0, keepdims=True) for a in range(1, 8)]
    counts += [picked[72 + a:73 + a] for a in range(8)]
    return counts, z


def _peer_kernel(x_ref, at_ref, sg_ref, wa_ref, wg_ref, nw_ref, wq_ref, keys_ref, u_ref, vt_ref, fw_ref, o_ref,
                 x1_ref, hn_ref, q_ref, l_ref, nb_ref, r_ref, rk_ref, a_ref, h_ref, y_ref, *, tm, te):
    g = pl.program_id(1)
    n_chunks = tm // LANES
    n_tok = tm // PEER_TOK
    chunks_per_tok = PEER_TOK // LANES
    i_per_tile = te // PEER_N_KEYS
    nt = (((1,), (1,)), ((), ()))

    @pl.when(g == 0)
    def _select():
        y_ref[...] = jnp.zeros(y_ref.shape, F32)
        x1 = (x_ref[...] + jnp.dot(at_ref[...], wa_ref[...], preferred_element_type=F32)
              + jnp.dot(sg_ref[...], wg_ref[...], preferred_element_type=F32))
        x1_ref[...] = x1
        hn_ref[...] = _rms(x1, nw_ref[...]).astype(BF16)
        q = jnp.dot(hn_ref[...], wq_ref[...], preferred_element_type=F32).astype(BF16)
        for hh in range(PEER_HEADS):
            q_ref[hh] = q[:, hh * PEER_QUERY_DIM:(hh + 1) * PEER_QUERY_DIM]
        key_idx = lax.broadcasted_iota(jnp.int32, (PEER_N_KEYS, LANES), 0).astype(F32)
        flat = _cand_flat_index()

        def select_chunk(hh, lc, break_ties):
            rows = pl.ds(pl.multiple_of(lc * LANES, LANES), LANES)
            s1 = lax.dot_general(keys_ref[0], q_ref[hh, rows, 0:PEER_HALF], nt,
                                 preferred_element_type=F32)
            s2 = lax.dot_general(keys_ref[1], q_ref[hh, rows, PEER_HALF:PEER_QUERY_DIM], nt,
                                 preferred_element_type=F32)
            rank1, v1, in1 = _top16(s1, key_idx, break_ties)
            rank2, v2, in2 = _top16(s2, key_idx, break_ties)
            counts, z = _select_pairs(v1, v2, flat, break_ties)
            nb = jnp.zeros(s1.shape, F32)
            for a in range(PEER_TOPK):
                at_round = (rank1 == float(a)) if break_ties else (s1 == v1[a:a + 1])
                nb = jnp.where(at_round, counts[a], nb)
            nb_ref[hh, lc] = nb
            rk_ref[hh, lc] = rank2
            l_ref[hh, lc] = jnp.where(in1, 0.5 * jnp.exp(s1 - v1[0:1]), 0.0)
            r_ref[hh, lc] = jnp.where(in2, jnp.exp(s2 - v2[0:1]), 0.0) / z
            taken = [jnp.sum(jnp.where(in1, 1.0, 0.0), axis=0, keepdims=True),
                     jnp.sum(jnp.where(in2, 1.0, 0.0), axis=0, keepdims=True), sum(counts)]
            return sum(jnp.sum(jnp.where(t == float(PEER_TOPK), 0.0, 1.0)) for t in taken)

        def select(idx, carry):
            hh = idx // (n_chunks // SELECT_UNROLL)
            lc0 = (idx % (n_chunks // SELECT_UNROLL)) * SELECT_UNROLL
            tokens_with_ties = sum(select_chunk(hh, lc0 + k, False) for k in range(SELECT_UNROLL))

            @pl.when(tokens_with_ties > 0.0)
            def _redo_in_top_k_order():
                for k in range(SELECT_UNROLL):
                    select_chunk(hh, lc0 + k, True)

            return carry

        lax.fori_loop(0, PEER_HEADS * n_chunks // SELECT_UNROLL, select, 0)

    i0 = pl.multiple_of(g * i_per_tile, i_per_tile)

    @pl.when(g >= 0)
    def _activations():
        for th in range(n_tok):
            a_ref[th] = lax.dot_general(u_ref[...], hn_ref[th * PEER_TOK:(th + 1) * PEER_TOK, :], nt,
                                        preferred_element_type=F32)

    @pl.when(g >= -1)
    def _gate():
        for lc in range(n_chunks):
            th, cl = divmod(lc, chunks_per_tok)
            lanes = slice(cl * LANES, (cl + 1) * LANES)
            for ii in range(i_per_tile):
                w = None
                for hh in range(PEER_HEADS):
                    nb_rows = nb_ref[hh, lc, pl.ds(i0, i_per_tile), :]
                    l_rows = l_ref[hh, lc, pl.ds(i0, i_per_tile), :]
                    term = (jnp.where(rk_ref[hh, lc] < nb_rows[ii:ii + 1], r_ref[hh, lc], 0.0)
                            * l_rows[ii:ii + 1])
                    w = term if w is None else w + term
                rows = slice(ii * PEER_N_KEYS, (ii + 1) * PEER_N_KEYS)
                a = a_ref[th, rows, lanes]
                h_ref[th, rows, lanes] = (a * (1.0 + lax.erf(a * SQRT_HALF)) * w).astype(BF16)

    @pl.when(g >= -2)
    def _values():
        for th in range(n_tok):
            y_ref[:, th * PEER_TOK:(th + 1) * PEER_TOK] += jnp.dot(vt_ref[0], h_ref[th],
                                                                   preferred_element_type=F32)

    @pl.when(g == pl.num_programs(1) - 1)
    def _finish():
        x2 = x1_ref[...] + y_ref[...].T
        o_ref[...] = _rms(x2, fw_ref[...])


def _out_proj_peer(x2, attn, sgu, wa, wg, nw, wq, keys, u_tab, vt_tiles, fw, *, tm):
    T = x2.shape[0]
    te = PEER_TILE
    assert te % (8 * PEER_N_KEYS) == 0
    assert vt_tiles.shape == (PEER_N_EXPERTS // te, D_MODEL, te)
    n_chunks = tm // LANES
    n_tok = tm // PEER_TOK
    full = lambda shape: pl.BlockSpec(shape, lambda i, g: (0,) * len(shape), pipeline_mode=pl.Buffered(1))
    tok_tile = lambda width: pl.BlockSpec((tm, width), lambda i, g: (i, 0))
    first_step_tile = lambda width: pl.BlockSpec((tm, width), lambda i, g: (i, 0), pipeline_mode=pl.Buffered(1))
    sel_scratch = pltpu.VMEM((PEER_HEADS, n_chunks, PEER_N_KEYS, LANES), F32)
    return pl.pallas_call(
        functools.partial(_peer_kernel, tm=tm, te=te),
        grid=(T // tm, PEER_N_EXPERTS // te),
        in_specs=[
            first_step_tile(D_MODEL), first_step_tile(DIFF_WIDTH), first_step_tile(SGU_WIDTH),
            full((DIFF_WIDTH, D_MODEL)),
            full((SGU_WIDTH, D_MODEL)),
            full((1, D_MODEL)),
            full((D_MODEL, PEER_HEADS * PEER_QUERY_DIM)),
            full((2, PEER_N_KEYS, PEER_HALF)),
            pl.BlockSpec((te, D_MODEL), lambda i, g: (g, 0)),
            pl.BlockSpec((1, D_MODEL, te), lambda i, g: (g, 0, 0)),
            full((1, D_MODEL)),
        ],
        out_specs=tok_tile(D_MODEL),
        out_shape=jax.ShapeDtypeStruct((T, D_MODEL), F32),
        scratch_shapes=[pltpu.VMEM((tm, D_MODEL), F32), pltpu.VMEM((tm, D_MODEL), BF16),
                        pltpu.VMEM((PEER_HEADS, tm, PEER_QUERY_DIM), BF16),
                        sel_scratch, sel_scratch, sel_scratch, sel_scratch,
                        pltpu.VMEM((n_tok, te, PEER_TOK), F32), pltpu.VMEM((n_tok, te, PEER_TOK), BF16),
                        pltpu.VMEM((D_MODEL, tm), F32)],
        compiler_params=pltpu.CompilerParams(dimension_semantics=("arbitrary", "arbitrary"),
                                             vmem_limit_bytes=V7X_VMEM_LIMIT_BYTES),
        name="out_proj_peer",
    )(x2, attn, sgu, wa, wg, nw, wq, keys, u_tab, vt_tiles, fw)


def _rope_tables(seq):
    pos = jnp.arange(seq, dtype=F32)
    inv_freq = ROPE_THETA ** (-jnp.arange(0, DIFF_QK_DIM, 2, dtype=F32) / DIFF_QK_DIM)
    ang = pos[:, None] * inv_freq[None, :]
    cos = jnp.cos(ang)
    sin = jnp.sin(ang)
    reps = LANES // DIFF_QK_DIM
    cos_t = jnp.tile(jnp.concatenate([cos, cos], axis=-1), (1, reps))
    sin_t = jnp.tile(jnp.concatenate([-sin, sin], axis=-1), (1, reps))
    return cos_t, sin_t


def _tiles(batch, seq):
    tm = min(256, seq)
    tq = min(256, seq)
    t_peer = min(512, batch * seq)
    return tm, tq, t_peer


def kernel(x, attn_norm_w, w_in, lambda_q1, lambda_k1, lambda_q2, lambda_k2, subln_w, sgu_ln_w, sgu_ln_b,
           sgu_ws, sgu_b, sgu_out_norm_w, w_out, ffn_norm_w, peer_wq, peer_keys, peer_u, peer_v, final_norm_w):
    batch, seq, d_model = x.shape
    assert d_model == D_MODEL and seq % SGU_CHUNK == 0
    assert w_in.shape[0] == 1, "single-layer block: the PEER call folds in the closing RMSNorm"
    tm, tq, t_peer = _tiles(batch, seq)
    assert seq % tm == 0 and tm == tq and (batch * seq) % t_peer == 0
    cos_t, sin_t = _rope_tables(seq)
    x2 = x.reshape(batch * seq, D_MODEL)
    row = lambda w: w.reshape(1, -1)
    l = 0
    lam_init = 0.8 - 0.6 * math.exp(-0.3 * l)
    sb = jnp.repeat(sgu_b[l].T, HEAD_DIM, axis=1)
    q, k, vt, sgu = _in_proj(x2, row(attn_norm_w[l]), w_in[l].astype(BF16), cos_t, sin_t,
                             row(sgu_ln_w[l]), row(sgu_ln_b[l]), sgu_ws[l], sb, row(sgu_out_norm_w[l]),
                             seq=seq, tm=tm)
    lam_vecs = jnp.stack([lambda_q1[l], lambda_k1[l], lambda_q2[l], lambda_k2[l]])
    attn = _attention(q, k, vt, lam_vecs, jnp.tile(subln_w[l], SLAB_HEADS).reshape(LANES, 1),
                      batch=batch, seq=seq, tq=tq, lam_init=lam_init)
    wo = w_out[l].astype(BF16)
    vt_tiles = peer_v[l].astype(BF16).reshape(-1, PEER_TILE, D_MODEL).transpose(0, 2, 1)
    out = _out_proj_peer(x2, attn, sgu, wo[:DIFF_WIDTH], wo[DIFF_WIDTH:], row(ffn_norm_w[l]),
                         peer_wq[l].astype(BF16), peer_keys[l].astype(BF16), peer_u[l].astype(BF16),
                         vt_tiles, row(final_norm_w), tm=t_peer)
    return out.reshape(batch, seq, D_MODEL)
```

```python
import functools
import math

import jax
import jax.numpy as jnp
from jax import lax
from jax.experimental import pallas as pl
from jax.experimental.pallas import tpu as pltpu

F32 = jnp.float32
BF16 = jnp.bfloat16

D_MODEL = 1024
HEAD_DIM = 64
N_DIFF_HEADS = 8
DIFF_QK_DIM = HEAD_DIM // 2
DIFF_WIDTH = N_DIFF_HEADS * HEAD_DIM
N_SGU_HEADS = 8
SGU_WIDTH = N_SGU_HEADS * HEAD_DIM
IN_COLS = 3 * DIFF_WIDTH + 2 * SGU_WIDTH
SGU_CHUNK = 128
ROPE_THETA = 10000.0
PEER_HEADS = 8
PEER_N_KEYS = 128
PEER_N_EXPERTS = PEER_N_KEYS * PEER_N_KEYS
PEER_TOPK = 16
PEER_QUERY_DIM = 256
PEER_HALF = PEER_QUERY_DIM // 2
NORM_EPS = 1e-6
LN_EPS = 1e-5

LANES = 128
SLAB_HEADS = LANES // HEAD_DIM
N_SLABS = DIFF_WIDTH // LANES
ATTN_SLABS_PER_PASS = 4
DENOM_ROWS = 16
V7X_VMEM_LIMIT_BYTES = 60000 * 1024
NOT_SELECTED = 99.0
PEER_TILE = 2048
PEER_TOK = 256
SELECT_UNROLL = 4
NEG_INF = float("-inf")
SQRT_HALF = math.sqrt(0.5)


def _gelu(z):
    return 0.5 * z * (1.0 + lax.erf(z * SQRT_HALF))


def _rms(x, w):
    return x * lax.rsqrt(jnp.mean(x * x, axis=-1, keepdims=True) + NORM_EPS) * w


def _group_rms(o, w, low_half):
    ss = o * o
    s_lo = jnp.sum(jnp.where(low_half, ss, 0.0), axis=-1, keepdims=True)
    s_hi = jnp.sum(jnp.where(low_half, 0.0, ss), axis=-1, keepdims=True)
    ms = jnp.where(low_half, s_lo, s_hi) * (1.0 / HEAD_DIM)
    return o * lax.rsqrt(ms + NORM_EPS) * w


def _in_proj_kernel(x_ref, nw_ref, win_ref, cos_ref, sin_ref, lnw_ref, lnb_ref, ws_ref, sb_ref, onw_ref,
                    q_ref, k_ref, vt_ref, g_ref, *, tm):
    x = x_ref[...]
    h = _rms(x, nw_ref[...])
    proj = jnp.dot(h.astype(BF16), win_ref[...], preferred_element_type=F32)

    cos = cos_ref[...]
    sin = sin_ref[...]
    lane = lax.broadcasted_iota(jnp.int32, (tm, LANES), 1)
    first_half = (lane & (DIFF_QK_DIM // 2)) == 0

    def rope(t):
        partner = jnp.where(first_half, pltpu.roll(t, LANES - DIFF_QK_DIM // 2, 1),
                            pltpu.roll(t, DIFF_QK_DIM // 2, 1))
        return t * cos + partner * sin

    qk_scale = DIFF_QK_DIM ** -0.5 * math.log2(math.e)
    for c in range(N_SLABS):
        sl = slice(c * LANES, (c + 1) * LANES)
        q_ref[:, sl] = (rope(proj[:, sl]) * qk_scale).astype(BF16)
        k_ref[:, sl] = rope(proj[:, DIFF_WIDTH + c * LANES:DIFF_WIDTH + (c + 1) * LANES]).astype(BF16)
    vt_ref[0] = proj[:, 2 * DIFF_WIDTH:3 * DIFF_WIDTH].T.astype(BF16)

    u = _gelu(proj[:, 3 * DIFF_WIDTH:3 * DIFF_WIDTH + SGU_WIDTH])
    vg = _gelu(proj[:, 3 * DIFF_WIDTH + SGU_WIDTH:])
    mu = jnp.mean(vg, axis=-1, keepdims=True)
    xc = vg - mu
    vgn = (xc * lax.rsqrt(jnp.mean(xc * xc, axis=-1, keepdims=True) + LN_EPS) * lnw_ref[...]
           + lnb_ref[...]).astype(BF16)

    row = lax.broadcasted_iota(jnp.int32, (SGU_CHUNK, SGU_CHUNK), 0)
    col = lax.broadcasted_iota(jnp.int32, (SGU_CHUNK, SGU_CHUNK), 1)
    causal = row >= col
    w_heads = [jnp.where(causal, ws_ref[hh], 0.0).astype(BF16) for hh in range(N_SGU_HEADS)]
    low_half = lax.broadcasted_iota(jnp.int32, (SGU_CHUNK, LANES), 1) < HEAD_DIM

    for ch in range(tm // SGU_CHUNK):
        rows = slice(ch * SGU_CHUNK, (ch + 1) * SGU_CHUNK)
        for c in range(SGU_WIDTH // LANES):
            sl = slice(c * LANES, (c + 1) * LANES)
            vs = vgn[rows, sl]
            r_lo = jnp.dot(w_heads[SLAB_HEADS * c], vs, preferred_element_type=F32)
            r_hi = jnp.dot(w_heads[SLAB_HEADS * c + 1], vs, preferred_element_type=F32)
            mixed = jnp.where(low_half, r_lo, r_hi) + sb_ref[:, sl]
            o = u[rows, sl] * mixed
            g_ref[rows, sl] = _group_rms(o, onw_ref[:, sl], low_half).astype(BF16)


def _in_proj(x2, nw, win, cos_t, sin_t, lnw, lnb, ws, sb, onw, *, seq, tm):
    T = x2.shape[0]
    n_seq_tiles = seq // tm
    full = lambda shape: pl.BlockSpec(shape, lambda i: (0,) * len(shape))
    out = jax.ShapeDtypeStruct((T, DIFF_WIDTH), BF16)
    row_tile = pl.BlockSpec((tm, DIFF_WIDTH), lambda i: (i, 0))
    return pl.pallas_call(
        functools.partial(_in_proj_kernel, tm=tm),
        grid=(T // tm,),
        in_specs=[
            pl.BlockSpec((tm, D_MODEL), lambda i: (i, 0)),
            full((1, D_MODEL)),
            full((D_MODEL, IN_COLS)),
            pl.BlockSpec((tm, LANES), lambda i: (i % n_seq_tiles, 0)),
            pl.BlockSpec((tm, LANES), lambda i: (i % n_seq_tiles, 0)),
            full((1, SGU_WIDTH)),
            full((1, SGU_WIDTH)),
            full((N_SGU_HEADS, SGU_CHUNK, SGU_CHUNK)),
            full((SGU_CHUNK, SGU_WIDTH)),
            full((1, SGU_WIDTH)),
        ],
        out_specs=[row_tile, row_tile, pl.BlockSpec((1, DIFF_WIDTH, tm), lambda i: (i, 0, 0)), row_tile],
        out_shape=[out, out, jax.ShapeDtypeStruct((T // tm, DIFF_WIDTH, tm), BF16), out],
        compiler_params=pltpu.CompilerParams(dimension_semantics=("arbitrary",),
                                             vmem_limit_bytes=V7X_VMEM_LIMIT_BYTES),
        name="in_proj",
    )(x2, nw, win, cos_t, sin_t, lnw, lnb, ws, sb, onw)


def _attn_kernel(q_ref, k_ref, vt_ref, lam_ref, sw_ref, o_ref, m_ref, acc_ref, sa_ref, sb_ref,
                 *, tq, lam_init):
    qi = pl.program_id(1)
    lv = lam_ref[...]
    lam = (jnp.exp(jnp.sum(lv[0:1] * lv[1:2], axis=-1, keepdims=True))
           - jnp.exp(jnp.sum(lv[2:3] * lv[3:4], axis=-1, keepdims=True)) + lam_init)

    lane = lax.broadcasted_iota(jnp.int32, (tq, LANES), 1)
    key_pos = lax.broadcasted_iota(jnp.int32, (tq, tq), 0)
    qry_pos = lax.broadcasted_iota(jnp.int32, (tq, tq), 1)
    causal = key_pos <= qry_pos
    per_slab = 2 * SLAB_HEADS
    n_var = ATTN_SLABS_PER_PASS * per_slab
    nt = (((1,), (1,)), ((), ()))

    for c0 in range(0, N_SLABS, ATTN_SLABS_PER_PASS):
        slabs = [slice((c0 + u) * LANES, (c0 + u + 1) * LANES) for u in range(ATTN_SLABS_PER_PASS)]
        q_var = []
        for sl in slabs:
            qs = q_ref[:, sl]
            zero = jnp.zeros_like(qs)
            q_var += [jnp.where((lane >= x * DIFF_QK_DIM) & (lane < (x + 1) * DIFF_QK_DIM), qs, zero)
                      for x in range(per_slab)]
        m_ref[...] = jnp.full(m_ref.shape, NEG_INF, F32)
        acc_ref[...] = jnp.zeros(acc_ref.shape, F32)

        def score_tile(j, s_ref):
            rows = pl.ds(pl.multiple_of(j * tq, tq), tq)
            kts = [k_ref[rows, sl] for sl in slabs]
            for x in range(n_var):
                s_ref[x] = lax.dot_general(kts[x // per_slab], q_var[x], nt,
                                           preferred_element_type=F32)

        def softmax_pv(j, s_ref, diagonal):
            ones = jnp.ones((DENOM_ROWS, tq), BF16)
            vts = [jnp.concatenate([vt_ref[j, sl.start + hd * HEAD_DIM:sl.start + (hd + 1) * HEAD_DIM, :], ones],
                                   axis=0)
                   for sl in slabs for hd in range(SLAB_HEADS)]
            probs, alphas = [], []
            for x in range(n_var):
                s = s_ref[x]
                if diagonal:
                    s = jnp.where(causal, s, NEG_INF)
                m_prev = m_ref[x]
                m_next = jnp.maximum(m_prev, jnp.max(s, axis=0, keepdims=True))
                alpha = jnp.exp2(m_prev - m_next)
                p = jnp.exp2(s - m_next)
                m_ref[x] = m_next
                probs.append(p.astype(BF16))
                alphas.append(alpha)
            for x in range(n_var):
                acc_ref[x] = alphas[x] * acc_ref[x] + jnp.dot(vts[x // 2], probs[x],
                                                              preferred_element_type=F32)

        score_tile(0, sa_ref)

        def two_tiles(t, carry):
            j = 2 * t
            score_tile(j + 1, sb_ref)
            softmax_pv(j, sa_ref, False)
            score_tile(j + 2, sa_ref)
            softmax_pv(j + 1, sb_ref, False)
            return carry

        lax.fori_loop(0, qi // 2, two_tiles, 0)

        @pl.when(qi % 2 == 1)
        def _odd_tail():
            score_tile(qi, sb_ref)
            softmax_pv(qi - 1, sa_ref, False)
            softmax_pv(qi, sb_ref, True)

        @pl.when(qi % 2 == 0)
        def _even_tail():
            softmax_pv(qi, sa_ref, True)

        for u, sl in enumerate(slabs):
            x0 = u * per_slab
            part = [acc_ref[x0 + v, 0:HEAD_DIM, :] / acc_ref[x0 + v, HEAD_DIM:HEAD_DIM + 1, :]
                    for v in range(per_slab)]
            heads = []
            for hd in range(SLAB_HEADS):
                o = part[2 * hd] - lam * part[2 * hd + 1]
                ms = jnp.mean(o * o, axis=0, keepdims=True)
                heads.append(o * lax.rsqrt(ms + NORM_EPS))
            o = jnp.concatenate(heads, axis=0) * (sw_ref[...] * (1.0 - lam_init))
            o_ref[:, sl] = o.T.astype(BF16)


def _attention(q, k, vt, lam_vecs, sw, *, batch, seq, tq, lam_init):
    T = q.shape[0]
    nq = seq // tq
    n_var = ATTN_SLABS_PER_PASS * 2 * SLAB_HEADS
    assert vt.shape == (T // tq, DIFF_WIDTH, tq)
    return pl.pallas_call(
        functools.partial(_attn_kernel, tq=tq, lam_init=lam_init),
        grid=(batch, nq),
        in_specs=[
            pl.BlockSpec((tq, DIFF_WIDTH), lambda b, i: (b * nq + i, 0)),
            pl.BlockSpec((seq, DIFF_WIDTH), lambda b, i: (b, 0)),
            pl.BlockSpec((nq, DIFF_WIDTH, tq), lambda b, i: (b, 0, 0)),
            pl.BlockSpec((4, DIFF_QK_DIM), lambda b, i: (0, 0)),
            pl.BlockSpec((LANES, 1), lambda b, i: (0, 0)),
        ],
        out_specs=pl.BlockSpec((tq, DIFF_WIDTH), lambda b, i: (b * nq + i, 0)),
        out_shape=jax.ShapeDtypeStruct((T, DIFF_WIDTH), BF16),
        scratch_shapes=[pltpu.VMEM((n_var, 1, tq), F32),
                        pltpu.VMEM((n_var, HEAD_DIM + DENOM_ROWS, tq), F32), pltpu.VMEM((n_var, tq, tq), F32),
                        pltpu.VMEM((n_var, tq, tq), F32)],
        compiler_params=pltpu.CompilerParams(dimension_semantics=("arbitrary", "arbitrary"),
                                             vmem_limit_bytes=V7X_VMEM_LIMIT_BYTES),
        name="diff_attn",
    )(q, k, vt, lam_vecs, sw)


def _top16(s, key_idx, break_ties):
    cur = s
    rank = jnp.full(s.shape, NOT_SELECTED, F32)
    vals = jnp.zeros((PEER_TOPK, s.shape[1]), F32)
    slot = lax.broadcasted_iota(jnp.int32, vals.shape, 0)
    for it in range(PEER_TOPK):
        mx = jnp.max(cur, axis=0, keepdims=True)
        sel = cur == mx
        if break_ties:
            first = jnp.min(jnp.where(sel, key_idx, float(PEER_N_KEYS)), axis=0, keepdims=True)
            sel = key_idx == first
        cur = jnp.where(sel, NEG_INF, cur)
        rank = jnp.where(sel, float(it), rank)
        vals = jnp.where(slot == it, mx, vals)
    return rank, vals, cur == NEG_INF


_CAND_ROWS = 16 + 7 * 8 + 8


def _cand_flat_index():
    r = lax.broadcasted_iota(jnp.int32, (_CAND_ROWS, LANES), 0)
    a = jnp.where(r < 16, 0, jnp.where(r < 72, 1 + ((r - 16) >> 3), 8 + (r - 72)))
    b = jnp.where(r < 16, r, jnp.where(r < 72, (r - 16) & 7, 0))
    return (a * PEER_TOPK + b).astype(F32)


def _select_pairs(v1, v2, flat, break_ties):
    pieces = [v1[0:1] + v2]
    pieces += [v1[a:a + 1] + v2[0:8] for a in range(1, 8)]
    pieces += [v1[8:16] + v2[0:1]]
    cand = jnp.concatenate(pieces, axis=0)
    z = jnp.zeros((1, cand.shape[1]), F32)
    top = None
    for it in range(PEER_TOPK):
        mx = jnp.max(cand, axis=0, keepdims=True)
        if it == 0:
            top = mx
        sel = cand == mx
        if break_ties:
            first = jnp.min(jnp.where(sel, flat, 1e9), axis=0, keepdims=True)
            sel = flat == first
        cand = jnp.where(sel, NEG_INF, cand)
        z = z + jnp.exp(mx - top)
    picked = jnp.where(cand == NEG_INF, 1.0, 0.0)
    counts = [jnp.sum(picked[0:16], axis=0, keepdims=True)]
    counts += [jnp.sum(picked[16 + 8 * (a - 1):24 + 8 * (a - 1)], axis=0, keepdims=True) for a in range(1, 8)]
    counts += [picked[72 + a:73 + a] for a in range(8)]
    return counts, z


def _peer_kernel(x_ref, at_ref, sg_ref, wa_ref, wg_ref, nw_ref, wq_ref, keys_ref, u_ref, vt_ref, fw_ref, o_ref,
                 x1_ref, hn_ref, q_ref, l_ref, nb_ref, r_ref, rk_ref, a_ref, h_ref, y_ref, *, tm, te):
    g = pl.program_id(1)
    n_chunks = tm // LANES
    n_tok = tm // PEER_TOK
    chunks_per_tok = PEER_TOK // LANES
    i_per_tile = te // PEER_N_KEYS
    nt = (((1,), (1,)), ((), ()))

    @pl.when(g == 0)
    def _select():
        y_ref[...] = jnp.zeros(y_ref.shape, F32)
        x1 = (x_ref[...] + jnp.dot(at_ref[...], wa_ref[...], preferred_element_type=F32)
              + jnp.dot(sg_ref[...], wg_ref[...], preferred_element_type=F32))
        x1_ref[...] = x1
        hn_ref[...] = _rms(x1, nw_ref[...]).astype(BF16)
        q = jnp.dot(hn_ref[...], wq_ref[...], preferred_element_type=F32).astype(BF16)
        for hh in range(PEER_HEADS):
            q_ref[hh] = q[:, hh * PEER_QUERY_DIM:(hh + 1) * PEER_QUERY_DIM]
        key_idx = lax.broadcasted_iota(jnp.int32, (PEER_N_KEYS, LANES), 0).astype(F32)
        flat = _cand_flat_index()

        def select_chunk(hh, lc, break_ties):
            rows = pl.ds(pl.multiple_of(lc * LANES, LANES), LANES)
            s1 = lax.dot_general(keys_ref[0], q_ref[hh, rows, 0:PEER_HALF], nt,
                                 preferred_element_type=F32)
            s2 = lax.dot_general(keys_ref[1], q_ref[hh, rows, PEER_HALF:PEER_QUERY_DIM], nt,
                                 preferred_element_type=F32)
            rank1, v1, in1 = _top16(s1, key_idx, break_ties)
            rank2, v2, in2 = _top16(s2, key_idx, break_ties)
            counts, z = _select_pairs(v1, v2, flat, break_ties)
            nb = jnp.zeros(s1.shape, F32)
            for a in range(PEER_TOPK):
                at_round = (rank1 == float(a)) if break_ties else (s1 == v1[a:a + 1])
                nb = jnp.where(at_round, counts[a], nb)
            nb_ref[hh, lc] = nb
            rk_ref[hh, lc] = rank2
            l_ref[hh, lc] = jnp.where(in1, 0.5 * jnp.exp(s1 - v1[0:1]), 0.0)
            r_ref[hh, lc] = jnp.where(in2, jnp.exp(s2 - v2[0:1]), 0.0) / z
            taken = [jnp.sum(jnp.where(in1, 1.0, 0.0), axis=0, keepdims=True),
                     jnp.sum(jnp.where(in2, 1.0, 0.0), axis=0, keepdims=True), sum(counts)]
            return sum(jnp.sum(jnp.where(t == float(PEER_TOPK), 0.0, 1.0)) for t in taken)

        def select(idx, carry):
            hh = idx // (n_chunks // SELECT_UNROLL)
            lc0 = (idx % (n_chunks // SELECT_UNROLL)) * SELECT_UNROLL
            tokens_with_ties = sum(select_chunk(hh, lc0 + k, False) for k in range(SELECT_UNROLL))

            @pl.when(tokens_with_ties > 0.0)
            def _redo_in_top_k_order():
                for k in range(SELECT_UNROLL):
                    select_chunk(hh, lc0 + k, True)

            return carry

        lax.fori_loop(0, PEER_HEADS * n_chunks // SELECT_UNROLL, select, 0)

    i0 = pl.multiple_of(g * i_per_tile, i_per_tile)

    @pl.when(g >= 0)
    def _activations():
        for th in range(n_tok):
            a_ref[th] = lax.dot_general(u_ref[...], hn_ref[th * PEER_TOK:(th + 1) * PEER_TOK, :], nt,
                                        preferred_element_type=F32)

    @pl.when(g >= -1)
    def _gate():
        for lc in range(n_chunks):
            th, cl = divmod(lc, chunks_per_tok)
            lanes = slice(cl * LANES, (cl + 1) * LANES)
            for ii in range(i_per_tile):
                w = None
                for hh in range(PEER_HEADS):
                    nb_rows = nb_ref[hh, lc, pl.ds(i0, i_per_tile), :]
                    l_rows = l_ref[hh, lc, pl.ds(i0, i_per_tile), :]
                    term = (jnp.where(rk_ref[hh, lc] < nb_rows[ii:ii + 1], r_ref[hh, lc], 0.0)
                            * l_rows[ii:ii + 1])
                    w = term if w is None else w + term
                rows = slice(ii * PEER_N_KEYS, (ii + 1) * PEER_N_KEYS)
                a = a_ref[th, rows, lanes]
                h_ref[th, rows, lanes] = (a * (1.0 + lax.erf(a * SQRT_HALF)) * w).astype(BF16)

    @pl.when(g >= -2)
    def _values():
        for th in range(n_tok):
            y_ref[:, th * PEER_TOK:(th + 1) * PEER_TOK] += jnp.dot(vt_ref[0], h_ref[th],
                                                                   preferred_element_type=F32)

    @pl.when(g == pl.num_programs(1) - 1)
    def _finish():
        x2 = x1_ref[...] + y_ref[...].T
        o_ref[...] = _rms(x2, fw_ref[...])


def _out_proj_peer(x2, attn, sgu, wa, wg, nw, wq, keys, u_tab, vt_tiles, fw, *, tm):
    T = x2.shape[0]
    te = PEER_TILE
    assert te % (8 * PEER_N_KEYS) == 0
    assert vt_tiles.shape == (PEER_N_EXPERTS // te, D_MODEL, te)
    n_chunks = tm // LANES
    n_tok = tm // PEER_TOK
    full = lambda shape: pl.BlockSpec(shape, lambda i, g: (0,) * len(shape), pipeline_mode=pl.Buffered(1))
    tok_tile = lambda width: pl.BlockSpec((tm, width), lambda i, g: (i, 0))
    first_step_tile = lambda width: pl.BlockSpec((tm, width), lambda i, g: (i, 0), pipeline_mode=pl.Buffered(1))
    sel_scratch = pltpu.VMEM((PEER_HEADS, n_chunks, PEER_N_KEYS, LANES), F32)
    return pl.pallas_call(
        functools.partial(_peer_kernel, tm=tm, te=te),
        grid=(T // tm, PEER_N_EXPERTS // te),
        in_specs=[
            first_step_tile(D_MODEL), first_step_tile(DIFF_WIDTH), first_step_tile(SGU_WIDTH),
            full((DIFF_WIDTH, D_MODEL)),
            full((SGU_WIDTH, D_MODEL)),
            full((1, D_MODEL)),
            full((D_MODEL, PEER_HEADS * PEER_QUERY_DIM)),
            full((2, PEER_N_KEYS, PEER_HALF)),
            pl.BlockSpec((te, D_MODEL), lambda i, g: (g, 0)),
            pl.BlockSpec((1, D_MODEL, te), lambda i, g: (g, 0, 0)),
            full((1, D_MODEL)),
        ],
        out_specs=tok_tile(D_MODEL),
        out_shape=jax.ShapeDtypeStruct((T, D_MODEL), F32),
        scratch_shapes=[pltpu.VMEM((tm, D_MODEL), F32), pltpu.VMEM((tm, D_MODEL), BF16),
                        pltpu.VMEM((PEER_HEADS, tm, PEER_QUERY_DIM), BF16),
                        sel_scratch, sel_scratch, sel_scratch, sel_scratch,
                        pltpu.VMEM((n_tok, te, PEER_TOK), F32), pltpu.VMEM((n_tok, te, PEER_TOK), BF16),
                        pltpu.VMEM((D_MODEL, tm), F32)],
        compiler_params=pltpu.CompilerParams(dimension_semantics=("arbitrary", "arbitrary"),
                                             vmem_limit_bytes=V7X_VMEM_LIMIT_BYTES),
        name="out_proj_peer",
    )(x2, attn, sgu, wa, wg, nw, wq, keys, u_tab, vt_tiles, fw)


def _rope_tables(seq):
    pos = jnp.arange(seq, dtype=F32)
    inv_freq = ROPE_THETA ** (-jnp.arange(0, DIFF_QK_DIM, 2, dtype=F32) / DIFF_QK_DIM)
    ang = pos[:, None] * inv_freq[None, :]
    cos = jnp.cos(ang)
    sin = jnp.sin(ang)
    reps = LANES // DIFF_QK_DIM
    cos_t = jnp.tile(jnp.concatenate([cos, cos], axis=-1), (1, reps))
    sin_t = jnp.tile(jnp.concatenate([-sin, sin], axis=-1), (1, reps))
    return cos_t, sin_t


def _tiles(batch, seq):
    tm = min(256, seq)
    tq = min(256, seq)
    t_peer = min(512, batch * seq)
    return tm, tq, t_peer


def kernel(x, attn_norm_w, w_in, lambda_q1, lambda_k1, lambda_q2, lambda_k2, subln_w, sgu_ln_w, sgu_ln_b,
           sgu_ws, sgu_b, sgu_out_norm_w, w_out, ffn_norm_w, peer_wq, peer_keys, peer_u, peer_v, final_norm_w):
    batch, seq, d_model = x.shape
    assert d_model == D_MODEL and seq % SGU_CHUNK == 0
    assert w_in.shape[0] == 1, "single-layer block: the PEER call folds in the closing RMSNorm"
    tm, tq, t_peer = _tiles(batch, seq)
    assert seq % tm == 0 and tm == tq and (batch * seq) % t_peer == 0
    cos_t, sin_t = _rope_tables(seq)
    x2 = x.reshape(batch * seq, D_MODEL)
    row = lambda w: w.reshape(1, -1)
    l = 0
    lam_init = 0.8 - 0.6 * math.exp(-0.3 * l)
    sb = jnp.repeat(sgu_b[l].T, HEAD_DIM, axis=1)
    q, k, vt, sgu = _in_proj(x2, row(attn_norm_w[l]), w_in[l].astype(BF16), cos_t, sin_t,
                             row(sgu_ln_w[l]), row(sgu_ln_b[l]), sgu_ws[l], sb, row(sgu_out_norm_w[l]),
                             seq=seq, tm=tm)
    lam_vecs = jnp.stack([lambda_q1[l], lambda_k1[l], lambda_q2[l], lambda_k2[l]])
    attn = _attention(q, k, vt, lam_vecs, jnp.tile(subln_w[l], SLAB_HEADS).reshape(LANES, 1),
                      batch=batch, seq=seq, tq=tq, lam_init=lam_init)
    wo = w_out[l].astype(BF16)
    vt_tiles = peer_v[l].astype(BF16).reshape(-1, PEER_TILE, D_MODEL).transpose(0, 2, 1)
    out = _out_proj_peer(x2, attn, sgu, wo[:DIFF_WIDTH], wo[DIFF_WIDTH:], row(ffn_norm_w[l]),
                         peer_wq[l].astype(BF16), peer_keys[l].astype(BF16), peer_u[l].astype(BF16),
                         vt_tiles, row(final_norm_w), tm=t_peer)
    return out.reshape(batch, seq, D_MODEL)
```

```python
import functools
import math

import jax
import jax.numpy as jnp
from jax import lax
from jax.experimental import pallas as pl
from jax.experimental.pallas import tpu as pltpu

F32 = jnp.float32
BF16 = jnp.bfloat16

D_MODEL = 1024
HEAD_DIM = 64
N_DIFF_HEADS = 8
DIFF_QK_DIM = HEAD_DIM // 2
DIFF_WIDTH = N_DIFF_HEADS * HEAD_DIM
N_SGU_HEADS = 8
SGU_WIDTH = N_SGU_HEADS * HEAD_DIM
IN_COLS = 3 * DIFF_WIDTH + 2 * SGU_WIDTH
SGU_CHUNK = 128
ROPE_THETA = 10000.0
PEER_HEADS = 8
PEER_N_KEYS = 128
PEER_N_EXPERTS = PEER_N_KEYS * PEER_N_KEYS
PEER_TOPK = 16
PEER_QUERY_DIM = 256
PEER_HALF = PEER_QUERY_DIM // 2
NORM_EPS = 1e-6
LN_EPS = 1e-5

LANES = 128
SLAB_HEADS = LANES // HEAD_DIM
N_SLABS = DIFF_WIDTH // LANES
ATTN_SLABS_PER_PASS = 4
DENOM_ROWS = 16
V7X_VMEM_LIMIT_BYTES = 60000 * 1024
NOT_SELECTED = 99.0
PEER_TILE = 2048
PEER_TOK = 256
SELECT_UNROLL = 4
NEG_INF = float("-inf")
SQRT_HALF = math.sqrt(0.5)


def _gelu(z):
    return 0.5 * z * (1.0 + lax.erf(z * SQRT_HALF))


def _rms(x, w):
    return x * lax.rsqrt(jnp.mean(x * x, axis=-1, keepdims=True) + NORM_EPS) * w


def _group_rms(o, w, low_half):
    ss = o * o
    s_lo = jnp.sum(jnp.where(low_half, ss, 0.0), axis=-1, keepdims=True)
    s_hi = jnp.sum(jnp.where(low_half, 0.0, ss), axis=-1, keepdims=True)
    ms = jnp.where(low_half, s_lo, s_hi) * (1.0 / HEAD_DIM)
    return o * lax.rsqrt(ms + NORM_EPS) * w


def _in_proj_kernel(x_ref, nw_ref, win_ref, cos_ref, sin_ref, lnw_ref, lnb_ref, ws_ref, sb_ref, onw_ref,
                    q_ref, k_ref, vt_ref, g_ref, *, tm, tk):
    x = x_ref[...]
    h = _rms(x, nw_ref[...])
    proj = jnp.dot(h.astype(BF16), win_ref[...], preferred_element_type=F32)

    cos = cos_ref[...]
    sin = sin_ref[...]
    lane = lax.broadcasted_iota(jnp.int32, (tm, LANES), 1)
    first_half = (lane & (DIFF_QK_DIM // 2)) == 0

    def rope(t):
        partner = jnp.where(first_half, pltpu.roll(t, LANES - DIFF_QK_DIM // 2, 1),
                            pltpu.roll(t, DIFF_QK_DIM // 2, 1))
        return t * cos + partner * sin

    qk_scale = DIFF_QK_DIM ** -0.5 * math.log2(math.e)
    for c in range(N_SLABS):
        sl = slice(c * LANES, (c + 1) * LANES)
        q_ref[:, sl] = (rope(proj[:, sl]) * qk_scale).astype(BF16)
        k_ref[:, sl] = rope(proj[:, DIFF_WIDTH + c * LANES:DIFF_WIDTH + (c + 1) * LANES]).astype(BF16)
    for u in range(tm // tk):
        vt_ref[u] = proj[u * tk:(u + 1) * tk, 2 * DIFF_WIDTH:3 * DIFF_WIDTH].T.astype(BF16)

    u = _gelu(proj[:, 3 * DIFF_WIDTH:3 * DIFF_WIDTH + SGU_WIDTH])
    vg = _gelu(proj[:, 3 * DIFF_WIDTH + SGU_WIDTH:])
    mu = jnp.mean(vg, axis=-1, keepdims=True)
    xc = vg - mu
    vgn = (xc * lax.rsqrt(jnp.mean(xc * xc, axis=-1, keepdims=True) + LN_EPS) * lnw_ref[...]
           + lnb_ref[...]).astype(BF16)

    row = lax.broadcasted_iota(jnp.int32, (SGU_CHUNK, SGU_CHUNK), 0)
    col = lax.broadcasted_iota(jnp.int32, (SGU_CHUNK, SGU_CHUNK), 1)
    causal = row >= col
    w_heads = [jnp.where(causal, ws_ref[hh], 0.0).astype(BF16) for hh in range(N_SGU_HEADS)]
    low_half = lax.broadcasted_iota(jnp.int32, (SGU_CHUNK, LANES), 1) < HEAD_DIM

    for ch in range(tm // SGU_CHUNK):
        rows = slice(ch * SGU_CHUNK, (ch + 1) * SGU_CHUNK)
        for c in range(SGU_WIDTH // LANES):
            sl = slice(c * LANES, (c + 1) * LANES)
            vs = vgn[rows, sl]
            r_lo = jnp.dot(w_heads[SLAB_HEADS * c], vs, preferred_element_type=F32)
            r_hi = jnp.dot(w_heads[SLAB_HEADS * c + 1], vs, preferred_element_type=F32)
            mixed = jnp.where(low_half, r_lo, r_hi) + sb_ref[:, sl]
            o = u[rows, sl] * mixed
            g_ref[rows, sl] = _group_rms(o, onw_ref[:, sl], low_half).astype(BF16)


def _in_proj(x2, nw, win, cos_t, sin_t, lnw, lnb, ws, sb, onw, *, seq, tm, tk):
    T = x2.shape[0]
    n_seq_tiles = seq // tm
    full = lambda shape: pl.BlockSpec(shape, lambda i: (0,) * len(shape))
    out = jax.ShapeDtypeStruct((T, DIFF_WIDTH), BF16)
    row_tile = pl.BlockSpec((tm, DIFF_WIDTH), lambda i: (i, 0))
    return pl.pallas_call(
        functools.partial(_in_proj_kernel, tm=tm, tk=tk),
        grid=(T // tm,),
        in_specs=[
            pl.BlockSpec((tm, D_MODEL), lambda i: (i, 0)),
            full((1, D_MODEL)),
            full((D_MODEL, IN_COLS)),
            pl.BlockSpec((tm, LANES), lambda i: (i % n_seq_tiles, 0)),
            pl.BlockSpec((tm, LANES), lambda i: (i % n_seq_tiles, 0)),
            full((1, SGU_WIDTH)),
            full((1, SGU_WIDTH)),
            full((N_SGU_HEADS, SGU_CHUNK, SGU_CHUNK)),
            full((SGU_CHUNK, SGU_WIDTH)),
            full((1, SGU_WIDTH)),
        ],
        out_specs=[row_tile, row_tile, pl.BlockSpec((tm // tk, DIFF_WIDTH, tk), lambda i: (i, 0, 0)), row_tile],
        out_shape=[out, out, jax.ShapeDtypeStruct((T // tk, DIFF_WIDTH, tk), BF16), out],
        compiler_params=pltpu.CompilerParams(dimension_semantics=("arbitrary",),
                                             vmem_limit_bytes=V7X_VMEM_LIMIT_BYTES),
        name="in_proj",
    )(x2, nw, win, cos_t, sin_t, lnw, lnb, ws, sb, onw)


def _attn_kernel(q_ref, k_ref, vt_ref, lam_ref, sw_ref, o_ref, m_ref, acc_ref, sa_ref, sb_ref,
                 *, tq, lam_init):
    qi = pl.program_id(1)
    lv = lam_ref[...]
    lam = (jnp.exp(jnp.sum(lv[0:1] * lv[1:2], axis=-1, keepdims=True))
           - jnp.exp(jnp.sum(lv[2:3] * lv[3:4], axis=-1, keepdims=True)) + lam_init)

    lane = lax.broadcasted_iota(jnp.int32, (tq, LANES), 1)
    key_pos = lax.broadcasted_iota(jnp.int32, (tq, tq), 0)
    qry_pos = lax.broadcasted_iota(jnp.int32, (tq, tq), 1)
    causal = key_pos <= qry_pos
    per_slab = 2 * SLAB_HEADS
    n_var = ATTN_SLABS_PER_PASS * per_slab
    nt = (((1,), (1,)), ((), ()))

    for c0 in range(0, N_SLABS, ATTN_SLABS_PER_PASS):
        slabs = [slice((c0 + u) * LANES, (c0 + u + 1) * LANES) for u in range(ATTN_SLABS_PER_PASS)]
        q_var = []
        for sl in slabs:
            qs = q_ref[:, sl]
            zero = jnp.zeros_like(qs)
            q_var += [jnp.where((lane >= x * DIFF_QK_DIM) & (lane < (x + 1) * DIFF_QK_DIM), qs, zero)
                      for x in range(per_slab)]
        m_ref[...] = jnp.full(m_ref.shape, NEG_INF, F32)
        acc_ref[...] = jnp.zeros(acc_ref.shape, F32)

        def score_tile(j, s_ref):
            rows = pl.ds(pl.multiple_of(j * tq, tq), tq)
            kts = [k_ref[rows, sl] for sl in slabs]
            for x in range(n_var):
                s_ref[x] = lax.dot_general(kts[x // per_slab], q_var[x], nt,
                                           preferred_element_type=F32)

        def softmax_pv(j, s_ref, diagonal):
            ones = jnp.ones((DENOM_ROWS, tq), BF16)
            vts = [jnp.concatenate([vt_ref[j, sl.start + hd * HEAD_DIM:sl.start + (hd + 1) * HEAD_DIM, :], ones],
                                   axis=0)
                   for sl in slabs for hd in range(SLAB_HEADS)]
            probs, alphas = [], []
            for x in range(n_var):
                s = s_ref[x]
                if diagonal:
                    s = jnp.where(causal, s, NEG_INF)
                m_prev = m_ref[x]
                m_next = jnp.maximum(m_prev, jnp.max(s, axis=0, keepdims=True))
                alpha = jnp.exp2(m_prev - m_next)
                p = jnp.exp2(s - m_next)
                m_ref[x] = m_next
                probs.append(p.astype(BF16))
                alphas.append(alpha)
            for x in range(n_var):
                acc_ref[x] = alphas[x] * acc_ref[x] + jnp.dot(vts[x // 2], probs[x],
                                                              preferred_element_type=F32)

        score_tile(0, sa_ref)

        def two_tiles(t, carry):
            j = 2 * t
            score_tile(j + 1, sb_ref)
            softmax_pv(j, sa_ref, False)
            score_tile(j + 2, sa_ref)
            softmax_pv(j + 1, sb_ref, False)
            return carry

        lax.fori_loop(0, qi // 2, two_tiles, 0)

        @pl.when(qi % 2 == 1)
        def _odd_tail():
            score_tile(qi, sb_ref)
            softmax_pv(qi - 1, sa_ref, False)
            softmax_pv(qi, sb_ref, True)

        @pl.when(qi % 2 == 0)
        def _even_tail():
            softmax_pv(qi, sa_ref, True)

        for u, sl in enumerate(slabs):
            x0 = u * per_slab
            part = [acc_ref[x0 + v, 0:HEAD_DIM, :] / acc_ref[x0 + v, HEAD_DIM:HEAD_DIM + 1, :]
                    for v in range(per_slab)]
            heads = []
            for hd in range(SLAB_HEADS):
                o = part[2 * hd] - lam * part[2 * hd + 1]
                ms = jnp.mean(o * o, axis=0, keepdims=True)
                heads.append(o * lax.rsqrt(ms + NORM_EPS))
            o = jnp.concatenate(heads, axis=0) * (sw_ref[...] * (1.0 - lam_init))
            o_ref[:, sl] = o.T.astype(BF16)


def _attention(q, k, vt, lam_vecs, sw, *, batch, seq, tq, lam_init):
    T = q.shape[0]
    nq = seq // tq
    n_var = ATTN_SLABS_PER_PASS * 2 * SLAB_HEADS
    assert vt.shape == (T // tq, DIFF_WIDTH, tq)
    return pl.pallas_call(
        functools.partial(_attn_kernel, tq=tq, lam_init=lam_init),
        grid=(batch, nq),
        in_specs=[
            pl.BlockSpec((tq, DIFF_WIDTH), lambda b, i: (b * nq + i, 0)),
            pl.BlockSpec((seq, DIFF_WIDTH), lambda b, i: (b, 0)),
            pl.BlockSpec((nq, DIFF_WIDTH, tq), lambda b, i: (b, 0, 0)),
            pl.BlockSpec((4, DIFF_QK_DIM), lambda b, i: (0, 0)),
            pl.BlockSpec((LANES, 1), lambda b, i: (0, 0)),
        ],
        out_specs=pl.BlockSpec((tq, DIFF_WIDTH), lambda b, i: (b * nq + i, 0)),
        out_shape=jax.ShapeDtypeStruct((T, DIFF_WIDTH), BF16),
        scratch_shapes=[pltpu.VMEM((n_var, 1, tq), F32),
                        pltpu.VMEM((n_var, HEAD_DIM + DENOM_ROWS, tq), F32), pltpu.VMEM((n_var, tq, tq), F32),
                        pltpu.VMEM((n_var, tq, tq), F32)],
        compiler_params=pltpu.CompilerParams(dimension_semantics=("arbitrary", "arbitrary"),
                                             vmem_limit_bytes=V7X_VMEM_LIMIT_BYTES),
        name="diff_attn",
    )(q, k, vt, lam_vecs, sw)


def _top16(s, key_idx, break_ties):
    cur = s
    rank = jnp.full(s.shape, NOT_SELECTED, F32)
    vals = jnp.zeros((PEER_TOPK, s.shape[1]), F32)
    slot = lax.broadcasted_iota(jnp.int32, vals.shape, 0)
    for it in range(PEER_TOPK):
        mx = jnp.max(cur, axis=0, keepdims=True)
        sel = cur == mx
        if break_ties:
            first = jnp.min(jnp.where(sel, key_idx, float(PEER_N_KEYS)), axis=0, keepdims=True)
            sel = key_idx == first
        cur = jnp.where(sel, NEG_INF, cur)
        rank = jnp.where(sel, float(it), rank)
        vals = jnp.where(slot == it, mx, vals)
    return rank, vals, cur == NEG_INF


_CAND_ROWS = 16 + 7 * 8 + 8


def _cand_flat_index():
    r = lax.broadcasted_iota(jnp.int32, (_CAND_ROWS, LANES), 0)
    a = jnp.where(r < 16, 0, jnp.where(r < 72, 1 + ((r - 16) >> 3), 8 + (r - 72)))
    b = jnp.where(r < 16, r, jnp.where(r < 72, (r - 16) & 7, 0))
    return (a * PEER_TOPK + b).astype(F32)


def _select_pairs(v1, v2, flat, break_ties):
    pieces = [v1[0:1] + v2]
    pieces += [v1[a:a + 1] + v2[0:8] for a in range(1, 8)]
    pieces += [v1[8:16] + v2[0:1]]
    cand = jnp.concatenate(pieces, axis=0)
    z = jnp.zeros((1, cand.shape[1]), F32)
    top = None
    for it in range(PEER_TOPK):
        mx = jnp.max(cand, axis=0, keepdims=True)
        if it == 0:
            top = mx
        sel = cand == mx
        if break_ties:
            first = jnp.min(jnp.where(sel, flat, 1e9), axis=0, keepdims=True)
            sel = flat == first
        cand = jnp.where(sel, NEG_INF, cand)
        z = z + jnp.exp(mx - top)
    picked = jnp.where(cand == NEG_INF, 1.0, 0.0)
    counts = [jnp.sum(picked[0:16], axis=0, keepdims=True)]
    counts += [jnp.sum(picked[16 + 8 * (a - 1):24 + 8 * (a - 1)], axis=0, keepdims=True) for a in range(1, 8)]
    counts += [picked[72 + a:73 + a] for a in range(8)]
    return counts, z


def _peer_kernel(x_ref, at_ref, sg_ref, wa_ref, wg_ref, nw_ref, wq_ref, keys_ref, u_ref, vt_ref, fw_ref, o_ref,
                 x1_ref, hn_ref, q_ref, l_ref, nb_ref, r_ref, rk_ref, a_ref, h_ref, y_ref, *, tm, te):
    g = pl.program_id(1)
    n_chunks = tm // LANES
    n_tok = tm // PEER_TOK
    chunks_per_tok = PEER_TOK // LANES
    i_per_tile = te // PEER_N_KEYS
    nt = (((1,), (1,)), ((), ()))

    @pl.when(g == 0)
    def _select():
        y_ref[...] = jnp.zeros(y_ref.shape, F32)
        x1 = (x_ref[...] + jnp.dot(at_ref[...], wa_ref[...], preferred_element_type=F32)
              + jnp.dot(sg_ref[...], wg_ref[...], preferred_element_type=F32))
        x1_ref[...] = x1
        hn_ref[...] = _rms(x1, nw_ref[...]).astype(BF16)
        q = jnp.dot(hn_ref[...], wq_ref[...], preferred_element_type=F32).astype(BF16)
        for hh in range(PEER_HEADS):
            q_ref[hh] = q[:, hh * PEER_QUERY_DIM:(hh + 1) * PEER_QUERY_DIM]
        key_idx = lax.broadcasted_iota(jnp.int32, (PEER_N_KEYS, LANES), 0).astype(F32)
        flat = _cand_flat_index()

        def select_chunk(hh, lc, break_ties):
            rows = pl.ds(pl.multiple_of(lc * LANES, LANES), LANES)
            s1 = lax.dot_general(keys_ref[0], q_ref[hh, rows, 0:PEER_HALF], nt,
                                 preferred_element_type=F32)
            s2 = lax.dot_general(keys_ref[1], q_ref[hh, rows, PEER_HALF:PEER_QUERY_DIM], nt,
                                 preferred_element_type=F32)
            rank1, v1, in1 = _top16(s1, key_idx, break_ties)
            rank2, v2, in2 = _top16(s2, key_idx, break_ties)
            counts, z = _select_pairs(v1, v2, flat, break_ties)
            nb = jnp.zeros(s1.shape, F32)
            for a in range(PEER_TOPK):
                at_round = (rank1 == float(a)) if break_ties else (s1 == v1[a:a + 1])
                nb = jnp.where(at_round, counts[a], nb)
            nb_ref[hh, lc] = nb
            rk_ref[hh, lc] = rank2
            l_ref[hh, lc] = jnp.where(in1, 0.5 * jnp.exp(s1 - v1[0:1]), 0.0)
            r_ref[hh, lc] = jnp.where(in2, jnp.exp(s2 - v2[0:1]), 0.0) / z
            taken = [jnp.sum(jnp.where(in1, 1.0, 0.0), axis=0, keepdims=True),
                     jnp.sum(jnp.where(in2, 1.0, 0.0), axis=0, keepdims=True), sum(counts)]
            return sum(jnp.sum(jnp.where(t == float(PEER_TOPK), 0.0, 1.0)) for t in taken)

        def select(idx, carry):
            hh = idx // (n_chunks // SELECT_UNROLL)
            lc0 = (idx % (n_chunks // SELECT_UNROLL)) * SELECT_UNROLL
            tokens_with_ties = sum(select_chunk(hh, lc0 + k, False) for k in range(SELECT_UNROLL))

            @pl.when(tokens_with_ties > 0.0)
            def _redo_in_top_k_order():
                for k in range(SELECT_UNROLL):
                    select_chunk(hh, lc0 + k, True)

            return carry

        lax.fori_loop(0, PEER_HEADS * n_chunks // SELECT_UNROLL, select, 0)

    i0 = pl.multiple_of(g * i_per_tile, i_per_tile)

    @pl.when(g >= 0)
    def _activations():
        for th in range(n_tok):
            a_ref[th] = lax.dot_general(u_ref[...], hn_ref[th * PEER_TOK:(th + 1) * PEER_TOK, :], nt,
                                        preferred_element_type=F32)

    @pl.when(g >= -1)
    def _gate():
        for lc in range(n_chunks):
            th, cl = divmod(lc, chunks_per_tok)
            lanes = slice(cl * LANES, (cl + 1) * LANES)
            for ii in range(i_per_tile):
                w = None
                for hh in range(PEER_HEADS):
                    nb_rows = nb_ref[hh, lc, pl.ds(i0, i_per_tile), :]
                    l_rows = l_ref[hh, lc, pl.ds(i0, i_per_tile), :]
                    term = (jnp.where(rk_ref[hh, lc] < nb_rows[ii:ii + 1], r_ref[hh, lc], 0.0)
                            * l_rows[ii:ii + 1])
                    w = term if w is None else w + term
                rows = slice(ii * PEER_N_KEYS, (ii + 1) * PEER_N_KEYS)
                a = a_ref[th, rows, lanes]
                h_ref[th, rows, lanes] = (a * (1.0 + lax.erf(a * SQRT_HALF)) * w).astype(BF16)

    @pl.when(g >= -2)
    def _values():
        for th in range(n_tok):
            y_ref[:, th * PEER_TOK:(th + 1) * PEER_TOK] += jnp.dot(vt_ref[0], h_ref[th],
                                                                   preferred_element_type=F32)

    @pl.when(g == pl.num_programs(1) - 1)
    def _finish():
        x2 = x1_ref[...] + y_ref[...].T
        o_ref[...] = _rms(x2, fw_ref[...])


def _out_proj_peer(x2, attn, sgu, wa, wg, nw, wq, keys, u_tab, vt_tiles, fw, *, tm):
    T = x2.shape[0]
    te = PEER_TILE
    assert te % (8 * PEER_N_KEYS) == 0
    assert vt_tiles.shape == (PEER_N_EXPERTS // te, D_MODEL, te)
    n_chunks = tm // LANES
    n_tok = tm // PEER_TOK
    full = lambda shape: pl.BlockSpec(shape, lambda i, g: (0,) * len(shape), pipeline_mode=pl.Buffered(1))
    tok_tile = lambda width: pl.BlockSpec((tm, width), lambda i, g: (i, 0))
    first_step_tile = lambda width: pl.BlockSpec((tm, width), lambda i, g: (i, 0), pipeline_mode=pl.Buffered(1))
    sel_scratch = pltpu.VMEM((PEER_HEADS, n_chunks, PEER_N_KEYS, LANES), F32)
    return pl.pallas_call(
        functools.partial(_peer_kernel, tm=tm, te=te),
        grid=(T // tm, PEER_N_EXPERTS // te),
        in_specs=[
            first_step_tile(D_MODEL), first_step_tile(DIFF_WIDTH), first_step_tile(SGU_WIDTH),
            full((DIFF_WIDTH, D_MODEL)),
            full((SGU_WIDTH, D_MODEL)),
            full((1, D_MODEL)),
            full((D_MODEL, PEER_HEADS * PEER_QUERY_DIM)),
            full((2, PEER_N_KEYS, PEER_HALF)),
            pl.BlockSpec((te, D_MODEL), lambda i, g: (g, 0)),
            pl.BlockSpec((1, D_MODEL, te), lambda i, g: (g, 0, 0)),
            full((1, D_MODEL)),
        ],
        out_specs=tok_tile(D_MODEL),
        out_shape=jax.ShapeDtypeStruct((T, D_MODEL), F32),
        scratch_shapes=[pltpu.VMEM((tm, D_MODEL), F32), pltpu.VMEM((tm, D_MODEL), BF16),
                        pltpu.VMEM((PEER_HEADS, tm, PEER_QUERY_DIM), BF16),
                        sel_scratch, sel_scratch, sel_scratch, sel_scratch,
                        pltpu.VMEM((n_tok, te, PEER_TOK), F32), pltpu.VMEM((n_tok, te, PEER_TOK), BF16),
                        pltpu.VMEM((D_MODEL, tm), F32)],
        compiler_params=pltpu.CompilerParams(dimension_semantics=("arbitrary", "arbitrary"),
                                             vmem_limit_bytes=V7X_VMEM_LIMIT_BYTES),
        name="out_proj_peer",
    )(x2, attn, sgu, wa, wg, nw, wq, keys, u_tab, vt_tiles, fw)


def _rope_tables(seq):
    pos = jnp.arange(seq, dtype=F32)
    inv_freq = ROPE_THETA ** (-jnp.arange(0, DIFF_QK_DIM, 2, dtype=F32) / DIFF_QK_DIM)
    ang = pos[:, None] * inv_freq[None, :]
    cos = jnp.cos(ang)
    sin = jnp.sin(ang)
    reps = LANES // DIFF_QK_DIM
    cos_t = jnp.tile(jnp.concatenate([cos, cos], axis=-1), (1, reps))
    sin_t = jnp.tile(jnp.concatenate([-sin, sin], axis=-1), (1, reps))
    return cos_t, sin_t


def _tiles(batch, seq):
    tm = min(512, seq)
    tq = min(256, seq)
    t_peer = min(512, batch * seq)
    return tm, tq, t_peer


def kernel(x, attn_norm_w, w_in, lambda_q1, lambda_k1, lambda_q2, lambda_k2, subln_w, sgu_ln_w, sgu_ln_b,
           sgu_ws, sgu_b, sgu_out_norm_w, w_out, ffn_norm_w, peer_wq, peer_keys, peer_u, peer_v, final_norm_w):
    batch, seq, d_model = x.shape
    assert d_model == D_MODEL and seq % SGU_CHUNK == 0
    assert w_in.shape[0] == 1, "single-layer block: the PEER call folds in the closing RMSNorm"
    tm, tq, t_peer = _tiles(batch, seq)
    assert seq % tm == 0 and tm % tq == 0 and (batch * seq) % t_peer == 0
    cos_t, sin_t = _rope_tables(seq)
    x2 = x.reshape(batch * seq, D_MODEL)
    row = lambda w: w.reshape(1, -1)
    l = 0
    lam_init = 0.8 - 0.6 * math.exp(-0.3 * l)
    sb = jnp.repeat(sgu_b[l].T, HEAD_DIM, axis=1)
    q, k, vt, sgu = _in_proj(x2, row(attn_norm_w[l]), w_in[l].astype(BF16), cos_t, sin_t,
                             row(sgu_ln_w[l]), row(sgu_ln_b[l]), sgu_ws[l], sb, row(sgu_out_norm_w[l]),
                             seq=seq, tm=tm, tk=tq)
    lam_vecs = jnp.stack([lambda_q1[l], lambda_k1[l], lambda_q2[l], lambda_k2[l]])
    attn = _attention(q, k, vt, lam_vecs, jnp.tile(subln_w[l], SLAB_HEADS).reshape(LANES, 1),
                      batch=batch, seq=seq, tq=tq, lam_init=lam_init)
    wo = w_out[l].astype(BF16)
    vt_tiles = peer_v[l].astype(BF16).reshape(-1, PEER_TILE, D_MODEL).transpose(0, 2, 1)
    out = _out_proj_peer(x2, attn, sgu, wo[:DIFF_WIDTH], wo[DIFF_WIDTH:], row(ffn_norm_w[l]),
                         peer_wq[l].astype(BF16), peer_keys[l].astype(BF16), peer_u[l].astype(BF16),
                         vt_tiles, row(final_norm_w), tm=t_peer)
    return out.reshape(batch, seq, D_MODEL)
```

```python
import functools
import math

import jax
import jax.numpy as jnp
from jax import lax
from jax.experimental import pallas as pl
from jax.experimental.pallas import tpu as pltpu

F32 = jnp.float32
BF16 = jnp.bfloat16

D_MODEL = 1024
HEAD_DIM = 64
N_DIFF_HEADS = 8
DIFF_QK_DIM = HEAD_DIM // 2
DIFF_WIDTH = N_DIFF_HEADS * HEAD_DIM
N_SGU_HEADS = 8
SGU_WIDTH = N_SGU_HEADS * HEAD_DIM
IN_COLS = 3 * DIFF_WIDTH + 2 * SGU_WIDTH
SGU_CHUNK = 128
ROPE_THETA = 10000.0
PEER_HEADS = 8
PEER_N_KEYS = 128
PEER_N_EXPERTS = PEER_N_KEYS * PEER_N_KEYS
PEER_TOPK = 16
PEER_QUERY_DIM = 256
PEER_HALF = PEER_QUERY_DIM // 2
NORM_EPS = 1e-6
LN_EPS = 1e-5

LANES = 128
SLAB_HEADS = LANES // HEAD_DIM
N_SLABS = DIFF_WIDTH // LANES
ATTN_SLABS_PER_PASS = 4
DENOM_ROWS = 16
V7X_VMEM_LIMIT_BYTES = 60000 * 1024
NOT_SELECTED = 99.0
PEER_TILE = 2048
PEER_TOK = 256
SELECT_UNROLL = 4
NEG_INF = float("-inf")
SQRT_HALF = math.sqrt(0.5)


def _gelu(z):
    return 0.5 * z * (1.0 + lax.erf(z * SQRT_HALF))


def _rms(x, w):
    return x * lax.rsqrt(jnp.mean(x * x, axis=-1, keepdims=True) + NORM_EPS) * w


def _group_rms(o, w, low_half):
    ss = o * o
    s_lo = jnp.sum(jnp.where(low_half, ss, 0.0), axis=-1, keepdims=True)
    s_hi = jnp.sum(jnp.where(low_half, 0.0, ss), axis=-1, keepdims=True)
    ms = jnp.where(low_half, s_lo, s_hi) * (1.0 / HEAD_DIM)
    return o * lax.rsqrt(ms + NORM_EPS) * w


def _in_proj_kernel(x_ref, nw_ref, win_ref, cos_ref, sin_ref, lnw_ref, lnb_ref, ws_ref, sb_ref, onw_ref,
                    q_ref, k_ref, vt_ref, g_ref, *, tm, tk):
    x = x_ref[...]
    h = _rms(x, nw_ref[...])
    proj = jnp.dot(h.astype(BF16), win_ref[...], preferred_element_type=F32)

    cos = cos_ref[...]
    sin = sin_ref[...]
    lane = lax.broadcasted_iota(jnp.int32, (tm, LANES), 1)
    first_half = (lane & (DIFF_QK_DIM // 2)) == 0

    def rope(t):
        partner = jnp.where(first_half, pltpu.roll(t, LANES - DIFF_QK_DIM // 2, 1),
                            pltpu.roll(t, DIFF_QK_DIM // 2, 1))
        return t * cos + partner * sin

    qk_scale = DIFF_QK_DIM ** -0.5 * math.log2(math.e)
    for c in range(N_SLABS):
        sl = slice(c * LANES, (c + 1) * LANES)
        q_ref[:, sl] = (rope(proj[:, sl]) * qk_scale).astype(BF16)
        k_ref[:, sl] = rope(proj[:, DIFF_WIDTH + c * LANES:DIFF_WIDTH + (c + 1) * LANES]).astype(BF16)
    for u in range(tm // tk):
        vt_ref[u] = proj[u * tk:(u + 1) * tk, 2 * DIFF_WIDTH:3 * DIFF_WIDTH].T.astype(BF16)

    u = _gelu(proj[:, 3 * DIFF_WIDTH:3 * DIFF_WIDTH + SGU_WIDTH])
    vg = _gelu(proj[:, 3 * DIFF_WIDTH + SGU_WIDTH:])
    mu = jnp.mean(vg, axis=-1, keepdims=True)
    xc = vg - mu
    vgn = (xc * lax.rsqrt(jnp.mean(xc * xc, axis=-1, keepdims=True) + LN_EPS) * lnw_ref[...]
           + lnb_ref[...]).astype(BF16)

    row = lax.broadcasted_iota(jnp.int32, (SGU_CHUNK, SGU_CHUNK), 0)
    col = lax.broadcasted_iota(jnp.int32, (SGU_CHUNK, SGU_CHUNK), 1)
    causal = row >= col
    w_heads = [jnp.where(causal, ws_ref[hh], 0.0).astype(BF16) for hh in range(N_SGU_HEADS)]
    low_half = lax.broadcasted_iota(jnp.int32, (SGU_CHUNK, LANES), 1) < HEAD_DIM

    for ch in range(tm // SGU_CHUNK):
        rows = slice(ch * SGU_CHUNK, (ch + 1) * SGU_CHUNK)
        for c in range(SGU_WIDTH // LANES):
            sl = slice(c * LANES, (c + 1) * LANES)
            vs = vgn[rows, sl]
            r_lo = jnp.dot(w_heads[SLAB_HEADS * c], vs, preferred_element_type=F32)
            r_hi = jnp.dot(w_heads[SLAB_HEADS * c + 1], vs, preferred_element_type=F32)
            mixed = jnp.where(low_half, r_lo, r_hi) + sb_ref[:, sl]
            o = u[rows, sl] * mixed
            g_ref[rows, sl] = _group_rms(o, onw_ref[:, sl], low_half).astype(BF16)


def _in_proj(x2, nw, win, cos_t, sin_t, lnw, lnb, ws, sb, onw, *, seq, tm, tk):
    T = x2.shape[0]
    n_seq_tiles = seq // tm
    full = lambda shape: pl.BlockSpec(shape, lambda i: (0,) * len(shape))
    out = jax.ShapeDtypeStruct((T, DIFF_WIDTH), BF16)
    row_tile = pl.BlockSpec((tm, DIFF_WIDTH), lambda i: (i, 0))
    return pl.pallas_call(
        functools.partial(_in_proj_kernel, tm=tm, tk=tk),
        grid=(T // tm,),
        in_specs=[
            pl.BlockSpec((tm, D_MODEL), lambda i: (i, 0)),
            full((1, D_MODEL)),
            full((D_MODEL, IN_COLS)),
            pl.BlockSpec((tm, LANES), lambda i: (i % n_seq_tiles, 0)),
            pl.BlockSpec((tm, LANES), lambda i: (i % n_seq_tiles, 0)),
            full((1, SGU_WIDTH)),
            full((1, SGU_WIDTH)),
            full((N_SGU_HEADS, SGU_CHUNK, SGU_CHUNK)),
            full((SGU_CHUNK, SGU_WIDTH)),
            full((1, SGU_WIDTH)),
        ],
        out_specs=[row_tile, row_tile, pl.BlockSpec((tm // tk, DIFF_WIDTH, tk), lambda i: (i, 0, 0)), row_tile],
        out_shape=[out, out, jax.ShapeDtypeStruct((T // tk, DIFF_WIDTH, tk), BF16), out],
        compiler_params=pltpu.CompilerParams(dimension_semantics=("arbitrary",),
                                             vmem_limit_bytes=V7X_VMEM_LIMIT_BYTES),
        name="in_proj",
    )(x2, nw, win, cos_t, sin_t, lnw, lnb, ws, sb, onw)


def _attn_kernel(q_ref, k_ref, vt_ref, lam_ref, sw_ref, o_ref, m_ref, acc_ref, sa_ref, sb_ref,
                 *, tq, lam_init):
    qi = pl.program_id(1)
    lv = lam_ref[...]
    lam = (jnp.exp(jnp.sum(lv[0:1] * lv[1:2], axis=-1, keepdims=True))
           - jnp.exp(jnp.sum(lv[2:3] * lv[3:4], axis=-1, keepdims=True)) + lam_init)

    lane = lax.broadcasted_iota(jnp.int32, (tq, LANES), 1)
    key_pos = lax.broadcasted_iota(jnp.int32, (tq, tq), 0)
    qry_pos = lax.broadcasted_iota(jnp.int32, (tq, tq), 1)
    causal = key_pos <= qry_pos
    per_slab = 2 * SLAB_HEADS
    n_var = ATTN_SLABS_PER_PASS * per_slab
    nt = (((1,), (1,)), ((), ()))

    for c0 in range(0, N_SLABS, ATTN_SLABS_PER_PASS):
        slabs = [slice((c0 + u) * LANES, (c0 + u + 1) * LANES) for u in range(ATTN_SLABS_PER_PASS)]
        q_var = []
        for sl in slabs:
            qs = q_ref[:, sl]
            zero = jnp.zeros_like(qs)
            q_var += [jnp.where((lane >= x * DIFF_QK_DIM) & (lane < (x + 1) * DIFF_QK_DIM), qs, zero)
                      for x in range(per_slab)]
        m_ref[...] = jnp.full(m_ref.shape, NEG_INF, F32)
        acc_ref[...] = jnp.zeros(acc_ref.shape, F32)

        def score_tile(j, s_ref):
            rows = pl.ds(pl.multiple_of(j * tq, tq), tq)
            kts = [k_ref[rows, sl] for sl in slabs]
            for x in range(n_var):
                s_ref[x] = lax.dot_general(kts[x // per_slab], q_var[x], nt,
                                           preferred_element_type=F32)

        def softmax_pv(j, s_ref, diagonal):
            ones = jnp.ones((DENOM_ROWS, tq), BF16)
            vts = [jnp.concatenate([vt_ref[j, sl.start + hd * HEAD_DIM:sl.start + (hd + 1) * HEAD_DIM, :], ones],
                                   axis=0)
                   for sl in slabs for hd in range(SLAB_HEADS)]
            probs, alphas = [], []
            for x in range(n_var):
                s = s_ref[x]
                if diagonal:
                    s = jnp.where(causal, s, NEG_INF)
                m_prev = m_ref[x]
                m_next = jnp.maximum(m_prev, jnp.max(s, axis=0, keepdims=True))
                alpha = jnp.exp2(m_prev - m_next)
                p = jnp.exp2(s - m_next)
                m_ref[x] = m_next
                probs.append(p.astype(BF16))
                alphas.append(alpha)
            for x in range(n_var):
                acc_ref[x] = alphas[x] * acc_ref[x] + jnp.dot(vts[x // 2], probs[x],
                                                              preferred_element_type=F32)

        score_tile(0, sa_ref)

        def two_tiles(t, carry):
            j = 2 * t
            score_tile(j + 1, sb_ref)
            softmax_pv(j, sa_ref, False)
            score_tile(j + 2, sa_ref)
            softmax_pv(j + 1, sb_ref, False)
            return carry

        lax.fori_loop(0, qi // 2, two_tiles, 0)

        @pl.when(qi % 2 == 1)
        def _odd_tail():
            score_tile(qi, sb_ref)
            softmax_pv(qi - 1, sa_ref, False)
            softmax_pv(qi, sb_ref, True)

        @pl.when(qi % 2 == 0)
        def _even_tail():
            softmax_pv(qi, sa_ref, True)

        for u, sl in enumerate(slabs):
            x0 = u * per_slab
            part = [acc_ref[x0 + v, 0:HEAD_DIM, :] / acc_ref[x0 + v, HEAD_DIM:HEAD_DIM + 1, :]
                    for v in range(per_slab)]
            heads = []
            for hd in range(SLAB_HEADS):
                o = part[2 * hd] - lam * part[2 * hd + 1]
                ms = jnp.mean(o * o, axis=0, keepdims=True)
                heads.append(o * lax.rsqrt(ms + NORM_EPS))
            o = jnp.concatenate(heads, axis=0) * (sw_ref[...] * (1.0 - lam_init))
            o_ref[:, sl] = o.T.astype(BF16)


def _attention(q, k, vt, lam_vecs, sw, *, batch, seq, tq, lam_init):
    T = q.shape[0]
    nq = seq // tq
    n_var = ATTN_SLABS_PER_PASS * 2 * SLAB_HEADS
    assert vt.shape == (T // tq, DIFF_WIDTH, tq)
    return pl.pallas_call(
        functools.partial(_attn_kernel, tq=tq, lam_init=lam_init),
        grid=(batch, nq),
        in_specs=[
            pl.BlockSpec((tq, DIFF_WIDTH), lambda b, i: (b * nq + i, 0)),
            pl.BlockSpec((seq, DIFF_WIDTH), lambda b, i: (b, 0)),
            pl.BlockSpec((nq, DIFF_WIDTH, tq), lambda b, i: (b, 0, 0)),
            pl.BlockSpec((4, DIFF_QK_DIM), lambda b, i: (0, 0)),
            pl.BlockSpec((LANES, 1), lambda b, i: (0, 0)),
        ],
        out_specs=pl.BlockSpec((tq, DIFF_WIDTH), lambda b, i: (b * nq + i, 0)),
        out_shape=jax.ShapeDtypeStruct((T, DIFF_WIDTH), BF16),
        scratch_shapes=[pltpu.VMEM((n_var, 1, tq), F32),
                        pltpu.VMEM((n_var, HEAD_DIM + DENOM_ROWS, tq), F32), pltpu.VMEM((n_var, tq, tq), F32),
                        pltpu.VMEM((n_var, tq, tq), F32)],
        compiler_params=pltpu.CompilerParams(dimension_semantics=("arbitrary", "arbitrary"),
                                             vmem_limit_bytes=V7X_VMEM_LIMIT_BYTES),
        name="diff_attn",
    )(q, k, vt, lam_vecs, sw)


def _top16(s, key_idx, break_ties):
    cur = s
    rank = jnp.full(s.shape, NOT_SELECTED, F32)
    vals = jnp.zeros((PEER_TOPK, s.shape[1]), F32)
    slot = lax.broadcasted_iota(jnp.int32, vals.shape, 0)
    for it in range(PEER_TOPK):
        mx = jnp.max(cur, axis=0, keepdims=True)
        sel = cur == mx
        if break_ties:
            first = jnp.min(jnp.where(sel, key_idx, float(PEER_N_KEYS)), axis=0, keepdims=True)
            sel = key_idx == first
        cur = jnp.where(sel, NEG_INF, cur)
        rank = jnp.where(sel, float(it), rank)
        vals = jnp.where(slot == it, mx, vals)
    return rank, vals, cur == NEG_INF


_CAND_ROWS = 16 + 7 * 8 + 8


def _cand_flat_index():
    r = lax.broadcasted_iota(jnp.int32, (_CAND_ROWS, LANES), 0)
    a = jnp.where(r < 16, 0, jnp.where(r < 72, 1 + ((r - 16) >> 3), 8 + (r - 72)))
    b = jnp.where(r < 16, r, jnp.where(r < 72, (r - 16) & 7, 0))
    return (a * PEER_TOPK + b).astype(F32)


def _select_pairs(v1, v2, flat, break_ties):
    pieces = [v1[0:1] + v2]
    pieces += [v1[a:a + 1] + v2[0:8] for a in range(1, 8)]
    pieces += [v1[8:16] + v2[0:1]]
    cand = jnp.concatenate(pieces, axis=0)
    z = jnp.zeros((1, cand.shape[1]), F32)
    top = None
    for it in range(PEER_TOPK):
        mx = jnp.max(cand, axis=0, keepdims=True)
        if it == 0:
            top = mx
        sel = cand == mx
        if break_ties:
            first = jnp.min(jnp.where(sel, flat, 1e9), axis=0, keepdims=True)
            sel = flat == first
        cand = jnp.where(sel, NEG_INF, cand)
        z = z + jnp.exp(mx - top)
    picked = jnp.where(cand == NEG_INF, 1.0, 0.0)
    counts = [jnp.sum(picked[0:16], axis=0, keepdims=True)]
    counts += [jnp.sum(picked[16 + 8 * (a - 1):24 + 8 * (a - 1)], axis=0, keepdims=True) for a in range(1, 8)]
    counts += [picked[72 + a:73 + a] for a in range(8)]
    return counts, z


def _peer_kernel(x_ref, at_ref, sg_ref, wa_ref, wg_ref, nw_ref, wq_ref, keys_ref, u_ref, vt_ref, fw_ref, o_ref,
                 x1_ref, hn_ref, q_ref, l_ref, nb_ref, r_ref, rk_ref, a_ref, h_ref, y_ref, *, tm, te):
    g = pl.program_id(1)
    n_chunks = tm // LANES
    n_tok = tm // PEER_TOK
    chunks_per_tok = PEER_TOK // LANES
    i_per_tile = te // PEER_N_KEYS
    nt = (((1,), (1,)), ((), ()))

    @pl.when(g == 0)
    def _select():
        y_ref[...] = jnp.zeros(y_ref.shape, F32)
        x1 = (x_ref[...] + jnp.dot(at_ref[...], wa_ref[...], preferred_element_type=F32)
              + jnp.dot(sg_ref[...], wg_ref[...], preferred_element_type=F32))
        x1_ref[...] = x1
        hn_ref[...] = _rms(x1, nw_ref[...]).astype(BF16)
        q = jnp.dot(hn_ref[...], wq_ref[...], preferred_element_type=F32).astype(BF16)
        for hh in range(PEER_HEADS):
            q_ref[hh] = q[:, hh * PEER_QUERY_DIM:(hh + 1) * PEER_QUERY_DIM]
        key_idx = lax.broadcasted_iota(jnp.int32, (PEER_N_KEYS, LANES), 0).astype(F32)
        flat = _cand_flat_index()

        def select_chunk(hh, lc, break_ties):
            rows = pl.ds(pl.multiple_of(lc * LANES, LANES), LANES)
            s1 = lax.dot_general(keys_ref[0], q_ref[hh, rows, 0:PEER_HALF], nt,
                                 preferred_element_type=F32)
            s2 = lax.dot_general(keys_ref[1], q_ref[hh, rows, PEER_HALF:PEER_QUERY_DIM], nt,
                                 preferred_element_type=F32)
            rank1, v1, in1 = _top16(s1, key_idx, break_ties)
            rank2, v2, in2 = _top16(s2, key_idx, break_ties)
            counts, z = _select_pairs(v1, v2, flat, break_ties)
            nb = jnp.zeros(s1.shape, F32)
            for a in range(PEER_TOPK):
                at_round = (rank1 == float(a)) if break_ties else (s1 == v1[a:a + 1])
                nb = jnp.where(at_round, counts[a], nb)
            nb_ref[hh, lc] = nb
            rk_ref[hh, lc] = rank2
            l_ref[hh, lc] = jnp.where(in1, 0.5 * jnp.exp(s1 - v1[0:1]), 0.0)
            r_ref[hh, lc] = jnp.where(in2, jnp.exp(s2 - v2[0:1]), 0.0) / z
            taken = [jnp.sum(jnp.where(in1, 1.0, 0.0), axis=0, keepdims=True),
                     jnp.sum(jnp.where(in2, 1.0, 0.0), axis=0, keepdims=True), sum(counts)]
            return sum(jnp.sum(jnp.where(t == float(PEER_TOPK), 0.0, 1.0)) for t in taken)

        def select(idx, carry):
            hh = idx // (n_chunks // SELECT_UNROLL)
            lc0 = (idx % (n_chunks // SELECT_UNROLL)) * SELECT_UNROLL
            tokens_with_ties = sum(select_chunk(hh, lc0 + k, False) for k in range(SELECT_UNROLL))

            @pl.when(tokens_with_ties > 0.0)
            def _redo_in_top_k_order():
                for k in range(SELECT_UNROLL):
                    select_chunk(hh, lc0 + k, True)

            return carry

        lax.fori_loop(0, PEER_HEADS * n_chunks // SELECT_UNROLL, select, 0)

    i0 = pl.multiple_of(g * i_per_tile, i_per_tile)

    @pl.when(g >= 0)
    def _activations():
        for th in range(n_tok):
            a = lax.dot_general(u_ref[...], hn_ref[th * PEER_TOK:(th + 1) * PEER_TOK, :], nt,
                                preferred_element_type=F32)
            a_ref[th] = a * (1.0 + lax.erf(a * SQRT_HALF))

    @pl.when(g >= -1)
    def _gate():
        for lc in range(n_chunks):
            th, cl = divmod(lc, chunks_per_tok)
            lanes = slice(cl * LANES, (cl + 1) * LANES)
            for ii in range(i_per_tile):
                w = None
                for hh in range(PEER_HEADS):
                    nb_rows = nb_ref[hh, lc, pl.ds(i0, i_per_tile), :]
                    l_rows = l_ref[hh, lc, pl.ds(i0, i_per_tile), :]
                    term = (jnp.where(rk_ref[hh, lc] < nb_rows[ii:ii + 1], r_ref[hh, lc], 0.0)
                            * l_rows[ii:ii + 1])
                    w = term if w is None else w + term
                rows = slice(ii * PEER_N_KEYS, (ii + 1) * PEER_N_KEYS)
                h_ref[th, rows, lanes] = (a_ref[th, rows, lanes] * w).astype(BF16)

    @pl.when(g >= -2)
    def _values():
        for th in range(n_tok):
            y_ref[:, th * PEER_TOK:(th + 1) * PEER_TOK] += jnp.dot(vt_ref[0], h_ref[th],
                                                                   preferred_element_type=F32)

    @pl.when(g == pl.num_programs(1) - 1)
    def _finish():
        x2 = x1_ref[...] + y_ref[...].T
        o_ref[...] = _rms(x2, fw_ref[...])


def _out_proj_peer(x2, attn, sgu, wa, wg, nw, wq, keys, u_tab, vt_tiles, fw, *, tm):
    T = x2.shape[0]
    te = PEER_TILE
    assert te % (8 * PEER_N_KEYS) == 0
    assert vt_tiles.shape == (PEER_N_EXPERTS // te, D_MODEL, te)
    n_chunks = tm // LANES
    n_tok = tm // PEER_TOK
    full = lambda shape: pl.BlockSpec(shape, lambda i, g: (0,) * len(shape), pipeline_mode=pl.Buffered(1))
    tok_tile = lambda width: pl.BlockSpec((tm, width), lambda i, g: (i, 0))
    first_step_tile = lambda width: pl.BlockSpec((tm, width), lambda i, g: (i, 0), pipeline_mode=pl.Buffered(1))
    sel_scratch = pltpu.VMEM((PEER_HEADS, n_chunks, PEER_N_KEYS, LANES), F32)
    return pl.pallas_call(
        functools.partial(_peer_kernel, tm=tm, te=te),
        grid=(T // tm, PEER_N_EXPERTS // te),
        in_specs=[
            first_step_tile(D_MODEL), first_step_tile(DIFF_WIDTH), first_step_tile(SGU_WIDTH),
            full((DIFF_WIDTH, D_MODEL)),
            full((SGU_WIDTH, D_MODEL)),
            full((1, D_MODEL)),
            full((D_MODEL, PEER_HEADS * PEER_QUERY_DIM)),
            full((2, PEER_N_KEYS, PEER_HALF)),
            pl.BlockSpec((te, D_MODEL), lambda i, g: (g, 0)),
            pl.BlockSpec((1, D_MODEL, te), lambda i, g: (g, 0, 0)),
            full((1, D_MODEL)),
        ],
        out_specs=tok_tile(D_MODEL),
        out_shape=jax.ShapeDtypeStruct((T, D_MODEL), F32),
        scratch_shapes=[pltpu.VMEM((tm, D_MODEL), F32), pltpu.VMEM((tm, D_MODEL), BF16),
                        pltpu.VMEM((PEER_HEADS, tm, PEER_QUERY_DIM), BF16),
                        sel_scratch, sel_scratch, sel_scratch, sel_scratch,
                        pltpu.VMEM((n_tok, te, PEER_TOK), F32), pltpu.VMEM((n_tok, te, PEER_TOK), BF16),
                        pltpu.VMEM((D_MODEL, tm), F32)],
        compiler_params=pltpu.CompilerParams(dimension_semantics=("arbitrary", "arbitrary"),
                                             vmem_limit_bytes=V7X_VMEM_LIMIT_BYTES),
        name="out_proj_peer",
    )(x2, attn, sgu, wa, wg, nw, wq, keys, u_tab, vt_tiles, fw)


def _rope_tables(seq):
    pos = jnp.arange(seq, dtype=F32)
    inv_freq = ROPE_THETA ** (-jnp.arange(0, DIFF_QK_DIM, 2, dtype=F32) / DIFF_QK_DIM)
    ang = pos[:, None] * inv_freq[None, :]
    cos = jnp.cos(ang)
    sin = jnp.sin(ang)
    reps = LANES // DIFF_QK_DIM
    cos_t = jnp.tile(jnp.concatenate([cos, cos], axis=-1), (1, reps))
    sin_t = jnp.tile(jnp.concatenate([-sin, sin], axis=-1), (1, reps))
    return cos_t, sin_t


def _tiles(batch, seq):
    tm = min(512, seq)
    tq = min(256, seq)
    t_peer = min(512, batch * seq)
    return tm, tq, t_peer


def kernel(x, attn_norm_w, w_in, lambda_q1, lambda_k1, lambda_q2, lambda_k2, subln_w, sgu_ln_w, sgu_ln_b,
           sgu_ws, sgu_b, sgu_out_norm_w, w_out, ffn_norm_w, peer_wq, peer_keys, peer_u, peer_v, final_norm_w):
    batch, seq, d_model = x.shape
    assert d_model == D_MODEL and seq % SGU_CHUNK == 0
    assert w_in.shape[0] == 1, "single-layer block: the PEER call folds in the closing RMSNorm"
    tm, tq, t_peer = _tiles(batch, seq)
    assert seq % tm == 0 and tm % tq == 0 and (batch * seq) % t_peer == 0
    cos_t, sin_t = _rope_tables(seq)
    x2 = x.reshape(batch * seq, D_MODEL)
    row = lambda w: w.reshape(1, -1)
    l = 0
    lam_init = 0.8 - 0.6 * math.exp(-0.3 * l)
    sb = jnp.repeat(sgu_b[l].T, HEAD_DIM, axis=1)
    q, k, vt, sgu = _in_proj(x2, row(attn_norm_w[l]), w_in[l].astype(BF16), cos_t, sin_t,
                             row(sgu_ln_w[l]), row(sgu_ln_b[l]), sgu_ws[l], sb, row(sgu_out_norm_w[l]),
                             seq=seq, tm=tm, tk=tq)
    lam_vecs = jnp.stack([lambda_q1[l], lambda_k1[l], lambda_q2[l], lambda_k2[l]])
    attn = _attention(q, k, vt, lam_vecs, jnp.tile(subln_w[l], SLAB_HEADS).reshape(LANES, 1),
                      batch=batch, seq=seq, tq=tq, lam_init=lam_init)
    wo = w_out[l].astype(BF16)
    vt_tiles = peer_v[l].astype(BF16).reshape(-1, PEER_TILE, D_MODEL).transpose(0, 2, 1)
    out = _out_proj_peer(x2, attn, sgu, wo[:DIFF_WIDTH], wo[DIFF_WIDTH:], row(ffn_norm_w[l]),
                         peer_wq[l].astype(BF16), peer_keys[l].astype(BF16), peer_u[l].astype(BF16),
                         vt_tiles, row(final_norm_w), tm=t_peer)
    return out.reshape(batch, seq, D_MODEL)
```

```python
import functools
import math

import jax
import jax.numpy as jnp
from jax import lax
from jax.experimental import pallas as pl
from jax.experimental.pallas import tpu as pltpu

F32 = jnp.float32
BF16 = jnp.bfloat16

D_MODEL = 1024
HEAD_DIM = 64
N_DIFF_HEADS = 8
DIFF_QK_DIM = HEAD_DIM // 2
DIFF_WIDTH = N_DIFF_HEADS * HEAD_DIM
N_SGU_HEADS = 8
SGU_WIDTH = N_SGU_HEADS * HEAD_DIM
IN_COLS = 3 * DIFF_WIDTH + 2 * SGU_WIDTH
SGU_CHUNK = 128
ROPE_THETA = 10000.0
PEER_HEADS = 8
PEER_N_KEYS = 128
PEER_N_EXPERTS = PEER_N_KEYS * PEER_N_KEYS
PEER_TOPK = 16
PEER_QUERY_DIM = 256
PEER_HALF = PEER_QUERY_DIM // 2
NORM_EPS = 1e-6
LN_EPS = 1e-5

LANES = 128
SLAB_HEADS = LANES // HEAD_DIM
N_SLABS = DIFF_WIDTH // LANES
ATTN_SLABS_PER_PASS = 4
DENOM_ROWS = 16
V7X_VMEM_LIMIT_BYTES = 60000 * 1024
NOT_SELECTED = 99.0
PEER_TILE = 2048
PEER_TOK = 256
SELECT_UNROLL = 4
NEG_INF = float("-inf")
SQRT_HALF = math.sqrt(0.5)


def _gelu(z):
    return 0.5 * z * (1.0 + lax.erf(z * SQRT_HALF))


def _rms(x, w):
    return x * lax.rsqrt(jnp.mean(x * x, axis=-1, keepdims=True) + NORM_EPS) * w


def _group_rms(o, w, low_half):
    ss = o * o
    s_lo = jnp.sum(jnp.where(low_half, ss, 0.0), axis=-1, keepdims=True)
    s_hi = jnp.sum(jnp.where(low_half, 0.0, ss), axis=-1, keepdims=True)
    ms = jnp.where(low_half, s_lo, s_hi) * (1.0 / HEAD_DIM)
    return o * lax.rsqrt(ms + NORM_EPS) * w


def _in_proj_kernel(x_ref, nw_ref, win_ref, cos_ref, sin_ref, lnw_ref, lnb_ref, ws_ref, sb_ref, onw_ref,
                    q_ref, k_ref, vt_ref, g_ref, *, tm, tk):
    x = x_ref[...]
    h = _rms(x, nw_ref[...])
    proj = jnp.dot(h.astype(BF16), win_ref[...], preferred_element_type=F32)

    cos = cos_ref[...]
    sin = sin_ref[...]
    lane = lax.broadcasted_iota(jnp.int32, (tm, LANES), 1)
    first_half = (lane & (DIFF_QK_DIM // 2)) == 0

    def rope(t):
        partner = jnp.where(first_half, pltpu.roll(t, LANES - DIFF_QK_DIM // 2, 1),
                            pltpu.roll(t, DIFF_QK_DIM // 2, 1))
        return t * cos + partner * sin

    qk_scale = DIFF_QK_DIM ** -0.5 * math.log2(math.e)
    for c in range(N_SLABS):
        sl = slice(c * LANES, (c + 1) * LANES)
        q_ref[:, sl] = (rope(proj[:, sl]) * qk_scale).astype(BF16)
        k_ref[:, sl] = rope(proj[:, DIFF_WIDTH + c * LANES:DIFF_WIDTH + (c + 1) * LANES]).astype(BF16)
    for u in range(tm // tk):
        vt_ref[u] = proj[u * tk:(u + 1) * tk, 2 * DIFF_WIDTH:3 * DIFF_WIDTH].T.astype(BF16)

    u = _gelu(proj[:, 3 * DIFF_WIDTH:3 * DIFF_WIDTH + SGU_WIDTH])
    vg = _gelu(proj[:, 3 * DIFF_WIDTH + SGU_WIDTH:])
    mu = jnp.mean(vg, axis=-1, keepdims=True)
    xc = vg - mu
    vgn = (xc * lax.rsqrt(jnp.mean(xc * xc, axis=-1, keepdims=True) + LN_EPS) * lnw_ref[...]
           + lnb_ref[...]).astype(BF16)

    row = lax.broadcasted_iota(jnp.int32, (SGU_CHUNK, SGU_CHUNK), 0)
    col = lax.broadcasted_iota(jnp.int32, (SGU_CHUNK, SGU_CHUNK), 1)
    causal = row >= col
    w_heads = [jnp.where(causal, ws_ref[hh], 0.0).astype(BF16) for hh in range(N_SGU_HEADS)]
    low_half = lax.broadcasted_iota(jnp.int32, (SGU_CHUNK, LANES), 1) < HEAD_DIM

    for ch in range(tm // SGU_CHUNK):
        rows = slice(ch * SGU_CHUNK, (ch + 1) * SGU_CHUNK)
        for c in range(SGU_WIDTH // LANES):
            sl = slice(c * LANES, (c + 1) * LANES)
            vs = vgn[rows, sl]
            r_lo = jnp.dot(w_heads[SLAB_HEADS * c], vs, preferred_element_type=F32)
            r_hi = jnp.dot(w_heads[SLAB_HEADS * c + 1], vs, preferred_element_type=F32)
            mixed = jnp.where(low_half, r_lo, r_hi) + sb_ref[:, sl]
            o = u[rows, sl] * mixed
            g_ref[rows, sl] = _group_rms(o, onw_ref[:, sl], low_half).astype(BF16)


def _in_proj(x2, nw, win, cos_t, sin_t, lnw, lnb, ws, sb, onw, *, seq, tm, tk):
    T = x2.shape[0]
    n_seq_tiles = seq // tm
    full = lambda shape: pl.BlockSpec(shape, lambda i: (0,) * len(shape))
    out = jax.ShapeDtypeStruct((T, DIFF_WIDTH), BF16)
    row_tile = pl.BlockSpec((tm, DIFF_WIDTH), lambda i: (i, 0))
    return pl.pallas_call(
        functools.partial(_in_proj_kernel, tm=tm, tk=tk),
        grid=(T // tm,),
        in_specs=[
            pl.BlockSpec((tm, D_MODEL), lambda i: (i, 0)),
            full((1, D_MODEL)),
            full((D_MODEL, IN_COLS)),
            pl.BlockSpec((tm, LANES), lambda i: (i % n_seq_tiles, 0)),
            pl.BlockSpec((tm, LANES), lambda i: (i % n_seq_tiles, 0)),
            full((1, SGU_WIDTH)),
            full((1, SGU_WIDTH)),
            full((N_SGU_HEADS, SGU_CHUNK, SGU_CHUNK)),
            full((SGU_CHUNK, SGU_WIDTH)),
            full((1, SGU_WIDTH)),
        ],
        out_specs=[row_tile, row_tile, pl.BlockSpec((tm // tk, DIFF_WIDTH, tk), lambda i: (i, 0, 0)), row_tile],
        out_shape=[out, out, jax.ShapeDtypeStruct((T // tk, DIFF_WIDTH, tk), BF16), out],
        compiler_params=pltpu.CompilerParams(dimension_semantics=("arbitrary",),
                                             vmem_limit_bytes=V7X_VMEM_LIMIT_BYTES),
        name="in_proj",
    )(x2, nw, win, cos_t, sin_t, lnw, lnb, ws, sb, onw)


def _attn_kernel(q_ref, k_ref, vt_ref, lam_ref, sw_ref, o_ref, m_ref, acc_ref, sa_ref, sb_ref,
                 *, tq, lam_init):
    qi = pl.program_id(1)
    lv = lam_ref[...]
    lam = (jnp.exp(jnp.sum(lv[0:1] * lv[1:2], axis=-1, keepdims=True))
           - jnp.exp(jnp.sum(lv[2:3] * lv[3:4], axis=-1, keepdims=True)) + lam_init)

    lane = lax.broadcasted_iota(jnp.int32, (tq, LANES), 1)
    key_pos = lax.broadcasted_iota(jnp.int32, (tq, tq), 0)
    qry_pos = lax.broadcasted_iota(jnp.int32, (tq, tq), 1)
    causal = key_pos <= qry_pos
    per_slab = 2 * SLAB_HEADS
    n_var = ATTN_SLABS_PER_PASS * per_slab
    nt = (((1,), (1,)), ((), ()))

    for c0 in range(0, N_SLABS, ATTN_SLABS_PER_PASS):
        slabs = [slice((c0 + u) * LANES, (c0 + u + 1) * LANES) for u in range(ATTN_SLABS_PER_PASS)]
        q_var = []
        for sl in slabs:
            qs = q_ref[:, sl]
            zero = jnp.zeros_like(qs)
            q_var += [jnp.where((lane >= x * DIFF_QK_DIM) & (lane < (x + 1) * DIFF_QK_DIM), qs, zero)
                      for x in range(per_slab)]
        m_ref[...] = jnp.full(m_ref.shape, NEG_INF, F32)
        acc_ref[...] = jnp.zeros(acc_ref.shape, F32)

        def score_tile(j, s_ref):
            rows = pl.ds(pl.multiple_of(j * tq, tq), tq)
            kts = [k_ref[rows, sl] for sl in slabs]
            for x in range(n_var):
                s_ref[x] = lax.dot_general(kts[x // per_slab], q_var[x], nt,
                                           preferred_element_type=F32)

        def softmax_pv(j, s_ref, diagonal):
            ones = jnp.ones((DENOM_ROWS, tq), BF16)
            vts = [jnp.concatenate([vt_ref[j, sl.start + hd * HEAD_DIM:sl.start + (hd + 1) * HEAD_DIM, :], ones],
                                   axis=0)
                   for sl in slabs for hd in range(SLAB_HEADS)]
            probs, alphas = [], []
            for x in range(n_var):
                s = s_ref[x]
                if diagonal:
                    s = jnp.where(causal, s, NEG_INF)
                m_prev = m_ref[x]
                m_next = jnp.maximum(m_prev, jnp.max(s, axis=0, keepdims=True))
                alpha = jnp.exp2(m_prev - m_next)
                p = jnp.exp2(s - m_next)
                m_ref[x] = m_next
                probs.append(p.astype(BF16))
                alphas.append(alpha)
            for x in range(n_var):
                acc_ref[x] = alphas[x] * acc_ref[x] + jnp.dot(vts[x // 2], probs[x],
                                                              preferred_element_type=F32)

        score_tile(0, sa_ref)

        def two_tiles(t, carry):
            j = 2 * t
            score_tile(j + 1, sb_ref)
            softmax_pv(j, sa_ref, False)
            score_tile(j + 2, sa_ref)
            softmax_pv(j + 1, sb_ref, False)
            return carry

        lax.fori_loop(0, qi // 2, two_tiles, 0)

        @pl.when(qi % 2 == 1)
        def _odd_tail():
            score_tile(qi, sb_ref)
            softmax_pv(qi - 1, sa_ref, False)
            softmax_pv(qi, sb_ref, True)

        @pl.when(qi % 2 == 0)
        def _even_tail():
            softmax_pv(qi, sa_ref, True)

        for u, sl in enumerate(slabs):
            x0 = u * per_slab
            part = [acc_ref[x0 + v, 0:HEAD_DIM, :] / acc_ref[x0 + v, HEAD_DIM:HEAD_DIM + 1, :]
                    for v in range(per_slab)]
            heads = []
            for hd in range(SLAB_HEADS):
                o = part[2 * hd] - lam * part[2 * hd + 1]
                ms = jnp.mean(o * o, axis=0, keepdims=True)
                heads.append(o * lax.rsqrt(ms + NORM_EPS))
            o = jnp.concatenate(heads, axis=0) * (sw_ref[...] * (1.0 - lam_init))
            o_ref[:, sl] = o.T.astype(BF16)


def _attention(q, k, vt, lam_vecs, sw, *, batch, seq, tq, lam_init):
    T = q.shape[0]
    nq = seq // tq
    n_var = ATTN_SLABS_PER_PASS * 2 * SLAB_HEADS
    assert vt.shape == (T // tq, DIFF_WIDTH, tq)
    return pl.pallas_call(
        functools.partial(_attn_kernel, tq=tq, lam_init=lam_init),
        grid=(batch, nq),
        in_specs=[
            pl.BlockSpec((tq, DIFF_WIDTH), lambda b, i: (b * nq + i, 0)),
            pl.BlockSpec((seq, DIFF_WIDTH), lambda b, i: (b, 0)),
            pl.BlockSpec((nq, DIFF_WIDTH, tq), lambda b, i: (b, 0, 0)),
            pl.BlockSpec((4, DIFF_QK_DIM), lambda b, i: (0, 0)),
            pl.BlockSpec((LANES, 1), lambda b, i: (0, 0)),
        ],
        out_specs=pl.BlockSpec((tq, DIFF_WIDTH), lambda b, i: (b * nq + i, 0)),
        out_shape=jax.ShapeDtypeStruct((T, DIFF_WIDTH), BF16),
        scratch_shapes=[pltpu.VMEM((n_var, 1, tq), F32),
                        pltpu.VMEM((n_var, HEAD_DIM + DENOM_ROWS, tq), F32), pltpu.VMEM((n_var, tq, tq), F32),
                        pltpu.VMEM((n_var, tq, tq), F32)],
        compiler_params=pltpu.CompilerParams(dimension_semantics=("arbitrary", "arbitrary"),
                                             vmem_limit_bytes=V7X_VMEM_LIMIT_BYTES),
        name="diff_attn",
    )(q, k, vt, lam_vecs, sw)


def _top16(s, key_idx, break_ties):
    cur = s
    rank = jnp.full(s.shape, NOT_SELECTED, F32)
    vals = jnp.zeros((PEER_TOPK, s.shape[1]), F32)
    slot = lax.broadcasted_iota(jnp.int32, vals.shape, 0)
    for it in range(PEER_TOPK):
        mx = jnp.max(cur, axis=0, keepdims=True)
        sel = cur == mx
        if break_ties:
            first = jnp.min(jnp.where(sel, key_idx, float(PEER_N_KEYS)), axis=0, keepdims=True)
            sel = key_idx == first
        cur = jnp.where(sel, NEG_INF, cur)
        rank = jnp.where(sel, float(it), rank)
        vals = jnp.where(slot == it, mx, vals)
    return rank, vals, cur == NEG_INF


_CAND_ROWS = 16 + 7 * 8 + 8


def _cand_flat_index():
    r = lax.broadcasted_iota(jnp.int32, (_CAND_ROWS, LANES), 0)
    a = jnp.where(r < 16, 0, jnp.where(r < 72, 1 + ((r - 16) >> 3), 8 + (r - 72)))
    b = jnp.where(r < 16, r, jnp.where(r < 72, (r - 16) & 7, 0))
    return (a * PEER_TOPK + b).astype(F32)


def _select_pairs(v1, v2, flat, break_ties):
    pieces = [v1[0:1] + v2]
    pieces += [v1[a:a + 1] + v2[0:8] for a in range(1, 8)]
    pieces += [v1[8:16] + v2[0:1]]
    cand = jnp.concatenate(pieces, axis=0)
    z = jnp.zeros((1, cand.shape[1]), F32)
    top = None
    for it in range(PEER_TOPK):
        mx = jnp.max(cand, axis=0, keepdims=True)
        if it == 0:
            top = mx
        sel = cand == mx
        if break_ties:
            first = jnp.min(jnp.where(sel, flat, 1e9), axis=0, keepdims=True)
            sel = flat == first
        cand = jnp.where(sel, NEG_INF, cand)
        z = z + jnp.exp(mx - top)
    picked = jnp.where(cand == NEG_INF, 1.0, 0.0)
    counts = [jnp.sum(picked[0:16], axis=0, keepdims=True)]
    counts += [jnp.sum(picked[16 + 8 * (a - 1):24 + 8 * (a - 1)], axis=0, keepdims=True) for a in range(1, 8)]
    counts += [picked[72 + a:73 + a] for a in range(8)]
    return counts, z


def _peer_kernel(x_ref, at_ref, sg_ref, wa_ref, wg_ref, nw_ref, wq_ref, keys_ref, u_ref, vt_ref, fw_ref, o_ref,
                 x1_ref, hn_ref, q_ref, l_ref, nb_ref, r_ref, rk_ref, w_ref, h_ref, y_ref, *, tm, te):
    g = pl.program_id(1)
    n_chunks = tm // LANES
    n_tok = tm // PEER_TOK
    chunks_per_tok = PEER_TOK // LANES
    i_per_tile = te // PEER_N_KEYS
    nt = (((1,), (1,)), ((), ()))

    @pl.when(g == 0)
    def _select():
        y_ref[...] = jnp.zeros(y_ref.shape, F32)
        x1 = (x_ref[...] + jnp.dot(at_ref[...], wa_ref[...], preferred_element_type=F32)
              + jnp.dot(sg_ref[...], wg_ref[...], preferred_element_type=F32))
        x1_ref[...] = x1
        hn_ref[...] = _rms(x1, nw_ref[...]).astype(BF16)
        q = jnp.dot(hn_ref[...], wq_ref[...], preferred_element_type=F32).astype(BF16)
        for hh in range(PEER_HEADS):
            q_ref[hh] = q[:, hh * PEER_QUERY_DIM:(hh + 1) * PEER_QUERY_DIM]
        key_idx = lax.broadcasted_iota(jnp.int32, (PEER_N_KEYS, LANES), 0).astype(F32)
        flat = _cand_flat_index()

        def select_chunk(hh, lc, break_ties):
            rows = pl.ds(pl.multiple_of(lc * LANES, LANES), LANES)
            s1 = lax.dot_general(keys_ref[0], q_ref[hh, rows, 0:PEER_HALF], nt,
                                 preferred_element_type=F32)
            s2 = lax.dot_general(keys_ref[1], q_ref[hh, rows, PEER_HALF:PEER_QUERY_DIM], nt,
                                 preferred_element_type=F32)
            rank1, v1, in1 = _top16(s1, key_idx, break_ties)
            rank2, v2, in2 = _top16(s2, key_idx, break_ties)
            counts, z = _select_pairs(v1, v2, flat, break_ties)
            nb = jnp.zeros(s1.shape, F32)
            for a in range(PEER_TOPK):
                at_round = (rank1 == float(a)) if break_ties else (s1 == v1[a:a + 1])
                nb = jnp.where(at_round, counts[a], nb)
            nb_ref[hh, lc] = nb
            rk_ref[hh, lc] = rank2
            l_ref[hh, lc] = jnp.where(in1, 0.5 * jnp.exp(s1 - v1[0:1]), 0.0)
            r_ref[hh, lc] = jnp.where(in2, jnp.exp(s2 - v2[0:1]), 0.0) / z
            taken = [jnp.sum(jnp.where(in1, 1.0, 0.0), axis=0, keepdims=True),
                     jnp.sum(jnp.where(in2, 1.0, 0.0), axis=0, keepdims=True), sum(counts)]
            return sum(jnp.sum(jnp.where(t == float(PEER_TOPK), 0.0, 1.0)) for t in taken)

        def select(idx, carry):
            hh = idx // (n_chunks // SELECT_UNROLL)
            lc0 = (idx % (n_chunks // SELECT_UNROLL)) * SELECT_UNROLL
            tokens_with_ties = sum(select_chunk(hh, lc0 + k, False) for k in range(SELECT_UNROLL))

            @pl.when(tokens_with_ties > 0.0)
            def _redo_in_top_k_order():
                for k in range(SELECT_UNROLL):
                    select_chunk(hh, lc0 + k, True)

            return carry

        lax.fori_loop(0, PEER_HEADS * n_chunks // SELECT_UNROLL, select, 0)

    i0 = pl.multiple_of(g * i_per_tile, i_per_tile)

    @pl.when(g >= 0)
    def _gate():
        for lc in range(n_chunks):
            th, cl = divmod(lc, chunks_per_tok)
            lanes = slice(cl * LANES, (cl + 1) * LANES)
            for ii in range(i_per_tile):
                w = None
                for hh in range(PEER_HEADS):
                    nb_rows = nb_ref[hh, lc, pl.ds(i0, i_per_tile), :]
                    l_rows = l_ref[hh, lc, pl.ds(i0, i_per_tile), :]
                    term = (jnp.where(rk_ref[hh, lc] < nb_rows[ii:ii + 1], r_ref[hh, lc], 0.0)
                            * l_rows[ii:ii + 1])
                    w = term if w is None else w + term
                w_ref[th, ii * PEER_N_KEYS:(ii + 1) * PEER_N_KEYS, lanes] = w

    @pl.when(g >= -1)
    def _activations():
        for th in range(n_tok):
            a = lax.dot_general(u_ref[...], hn_ref[th * PEER_TOK:(th + 1) * PEER_TOK, :], nt,
                                preferred_element_type=F32)
            h_ref[th] = (a * (1.0 + lax.erf(a * SQRT_HALF)) * w_ref[th]).astype(BF16)

    @pl.when(g >= -2)
    def _values():
        for th in range(n_tok):
            y_ref[:, th * PEER_TOK:(th + 1) * PEER_TOK] += jnp.dot(vt_ref[0], h_ref[th],
                                                                   preferred_element_type=F32)

    @pl.when(g == pl.num_programs(1) - 1)
    def _finish():
        x2 = x1_ref[...] + y_ref[...].T
        o_ref[...] = _rms(x2, fw_ref[...])


def _out_proj_peer(x2, attn, sgu, wa, wg, nw, wq, keys, u_tab, vt_tiles, fw, *, tm):
    T = x2.shape[0]
    te = PEER_TILE
    assert te % (8 * PEER_N_KEYS) == 0
    assert vt_tiles.shape == (PEER_N_EXPERTS // te, D_MODEL, te)
    n_chunks = tm // LANES
    n_tok = tm // PEER_TOK
    full = lambda shape: pl.BlockSpec(shape, lambda i, g: (0,) * len(shape), pipeline_mode=pl.Buffered(1))
    tok_tile = lambda width: pl.BlockSpec((tm, width), lambda i, g: (i, 0))
    first_step_tile = lambda width: pl.BlockSpec((tm, width), lambda i, g: (i, 0), pipeline_mode=pl.Buffered(1))
    sel_scratch = pltpu.VMEM((PEER_HEADS, n_chunks, PEER_N_KEYS, LANES), F32)
    return pl.pallas_call(
        functools.partial(_peer_kernel, tm=tm, te=te),
        grid=(T // tm, PEER_N_EXPERTS // te),
        in_specs=[
            first_step_tile(D_MODEL), first_step_tile(DIFF_WIDTH), first_step_tile(SGU_WIDTH),
            full((DIFF_WIDTH, D_MODEL)),
            full((SGU_WIDTH, D_MODEL)),
            full((1, D_MODEL)),
            full((D_MODEL, PEER_HEADS * PEER_QUERY_DIM)),
            full((2, PEER_N_KEYS, PEER_HALF)),
            pl.BlockSpec((te, D_MODEL), lambda i, g: (g, 0)),
            pl.BlockSpec((1, D_MODEL, te), lambda i, g: (g, 0, 0)),
            full((1, D_MODEL)),
        ],
        out_specs=tok_tile(D_MODEL),
        out_shape=jax.ShapeDtypeStruct((T, D_MODEL), F32),
        scratch_shapes=[pltpu.VMEM((tm, D_MODEL), F32), pltpu.VMEM((tm, D_MODEL), BF16),
                        pltpu.VMEM((PEER_HEADS, tm, PEER_QUERY_DIM), BF16),
                        sel_scratch, sel_scratch, sel_scratch, sel_scratch,
                        pltpu.VMEM((n_tok, te, PEER_TOK), F32), pltpu.VMEM((n_tok, te, PEER_TOK), BF16),
                        pltpu.VMEM((D_MODEL, tm), F32)],
        compiler_params=pltpu.CompilerParams(dimension_semantics=("arbitrary", "arbitrary"),
                                             vmem_limit_bytes=V7X_VMEM_LIMIT_BYTES),
        name="out_proj_peer",
    )(x2, attn, sgu, wa, wg, nw, wq, keys, u_tab, vt_tiles, fw)


def _rope_tables(seq):
    pos = jnp.arange(seq, dtype=F32)
    inv_freq = ROPE_THETA ** (-jnp.arange(0, DIFF_QK_DIM, 2, dtype=F32) / DIFF_QK_DIM)
    ang = pos[:, None] * inv_freq[None, :]
    cos = jnp.cos(ang)
    sin = jnp.sin(ang)
    reps = LANES // DIFF_QK_DIM
    cos_t = jnp.tile(jnp.concatenate([cos, cos], axis=-1), (1, reps))
    sin_t = jnp.tile(jnp.concatenate([-sin, sin], axis=-1), (1, reps))
    return cos_t, sin_t


def _tiles(batch, seq):
    tm = min(512, seq)
    tq = min(256, seq)
    t_peer = min(512, batch * seq)
    return tm, tq, t_peer


def kernel(x, attn_norm_w, w_in, lambda_q1, lambda_k1, lambda_q2, lambda_k2, subln_w, sgu_ln_w, sgu_ln_b,
           sgu_ws, sgu_b, sgu_out_norm_w, w_out, ffn_norm_w, peer_wq, peer_keys, peer_u, peer_v, final_norm_w):
    batch, seq, d_model = x.shape
    assert d_model == D_MODEL and seq % SGU_CHUNK == 0
    assert w_in.shape[0] == 1, "single-layer block: the PEER call folds in the closing RMSNorm"
    tm, tq, t_peer = _tiles(batch, seq)
    assert seq % tm == 0 and tm % tq == 0 and (batch * seq) % t_peer == 0
    cos_t, sin_t = _rope_tables(seq)
    x2 = x.reshape(batch * seq, D_MODEL)
    row = lambda w: w.reshape(1, -1)
    l = 0
    lam_init = 0.8 - 0.6 * math.exp(-0.3 * l)
    sb = jnp.repeat(sgu_b[l].T, HEAD_DIM, axis=1)
    q, k, vt, sgu = _in_proj(x2, row(attn_norm_w[l]), w_in[l].astype(BF16), cos_t, sin_t,
                             row(sgu_ln_w[l]), row(sgu_ln_b[l]), sgu_ws[l], sb, row(sgu_out_norm_w[l]),
                             seq=seq, tm=tm, tk=tq)
    lam_vecs = jnp.stack([lambda_q1[l], lambda_k1[l], lambda_q2[l], lambda_k2[l]])
    attn = _attention(q, k, vt, lam_vecs, jnp.tile(subln_w[l], SLAB_HEADS).reshape(LANES, 1),
                      batch=batch, seq=seq, tq=tq, lam_init=lam_init)
    wo = w_out[l].astype(BF16)
    vt_tiles = peer_v[l].astype(BF16).reshape(-1, PEER_TILE, D_MODEL).transpose(0, 2, 1)
    out = _out_proj_peer(x2, attn, sgu, wo[:DIFF_WIDTH], wo[DIFF_WIDTH:], row(ffn_norm_w[l]),
                         peer_wq[l].astype(BF16), peer_keys[l].astype(BF16), peer_u[l].astype(BF16),
                         vt_tiles, row(final_norm_w), tm=t_peer)
    return out.reshape(batch, seq, D_MODEL)
```

```python
import functools
import math

import jax
import jax.numpy as jnp
from jax import lax
from jax.experimental import pallas as pl
from jax.experimental.pallas import tpu as pltpu

F32 = jnp.float32
BF16 = jnp.bfloat16

D_MODEL = 1024
HEAD_DIM = 64
N_DIFF_HEADS = 8
DIFF_QK_DIM = HEAD_DIM // 2
DIFF_WIDTH = N_DIFF_HEADS * HEAD_DIM
N_SGU_HEADS = 8
SGU_WIDTH = N_SGU_HEADS * HEAD_DIM
IN_COLS = 3 * DIFF_WIDTH + 2 * SGU_WIDTH
SGU_CHUNK = 128
ROPE_THETA = 10000.0
PEER_HEADS = 8
PEER_N_KEYS = 128
PEER_N_EXPERTS = PEER_N_KEYS * PEER_N_KEYS
PEER_TOPK = 16
PEER_QUERY_DIM = 256
PEER_HALF = PEER_QUERY_DIM // 2
NORM_EPS = 1e-6
LN_EPS = 1e-5

LANES = 128
SLAB_HEADS = LANES // HEAD_DIM
N_SLABS = DIFF_WIDTH // LANES
ATTN_SLABS_PER_PASS = 4
DENOM_ROWS = 16
V7X_VMEM_LIMIT_BYTES = 60000 * 1024
NOT_SELECTED = 99.0
PEER_TILE = 2048
PEER_TOK = 256
SELECT_UNROLL = 4
NEG_INF = float("-inf")
SQRT_HALF = math.sqrt(0.5)


def _gelu(z):
    return 0.5 * z * (1.0 + lax.erf(z * SQRT_HALF))


def _rms(x, w):
    return x * lax.rsqrt(jnp.mean(x * x, axis=-1, keepdims=True) + NORM_EPS) * w


def _group_rms(o, w, low_half):
    ss = o * o
    s_lo = jnp.sum(jnp.where(low_half, ss, 0.0), axis=-1, keepdims=True)
    s_hi = jnp.sum(jnp.where(low_half, 0.0, ss), axis=-1, keepdims=True)
    ms = jnp.where(low_half, s_lo, s_hi) * (1.0 / HEAD_DIM)
    return o * lax.rsqrt(ms + NORM_EPS) * w


def _in_proj_kernel(x_ref, nw_ref, win_ref, cos_ref, sin_ref, lnw_ref, lnb_ref, ws_ref, sb_ref, onw_ref,
                    q_ref, k_ref, vt_ref, g_ref, *, tm, tk):
    x = x_ref[...]
    h = _rms(x, nw_ref[...])
    proj = jnp.dot(h.astype(BF16), win_ref[...], preferred_element_type=F32)

    cos = cos_ref[...]
    sin = sin_ref[...]
    lane = lax.broadcasted_iota(jnp.int32, (tm, LANES), 1)
    first_half = (lane & (DIFF_QK_DIM // 2)) == 0

    def rope(t):
        partner = jnp.where(first_half, pltpu.roll(t, LANES - DIFF_QK_DIM // 2, 1),
                            pltpu.roll(t, DIFF_QK_DIM // 2, 1))
        return t * cos + partner * sin

    qk_scale = DIFF_QK_DIM ** -0.5 * math.log2(math.e)
    for c in range(N_SLABS):
        sl = slice(c * LANES, (c + 1) * LANES)
        q_ref[:, sl] = (rope(proj[:, sl]) * qk_scale).astype(BF16)
        k_ref[:, sl] = rope(proj[:, DIFF_WIDTH + c * LANES:DIFF_WIDTH + (c + 1) * LANES]).astype(BF16)
    for u in range(tm // tk):
        vt_ref[u] = proj[u * tk:(u + 1) * tk, 2 * DIFF_WIDTH:3 * DIFF_WIDTH].T.astype(BF16)

    u = _gelu(proj[:, 3 * DIFF_WIDTH:3 * DIFF_WIDTH + SGU_WIDTH])
    vg = _gelu(proj[:, 3 * DIFF_WIDTH + SGU_WIDTH:])
    mu = jnp.mean(vg, axis=-1, keepdims=True)
    xc = vg - mu
    vgn = (xc * lax.rsqrt(jnp.mean(xc * xc, axis=-1, keepdims=True) + LN_EPS) * lnw_ref[...]
           + lnb_ref[...]).astype(BF16)

    row = lax.broadcasted_iota(jnp.int32, (SGU_CHUNK, SGU_CHUNK), 0)
    col = lax.broadcasted_iota(jnp.int32, (SGU_CHUNK, SGU_CHUNK), 1)
    causal = row >= col
    w_heads = [jnp.where(causal, ws_ref[hh], 0.0).astype(BF16) for hh in range(N_SGU_HEADS)]
    low_half = lax.broadcasted_iota(jnp.int32, (SGU_CHUNK, LANES), 1) < HEAD_DIM

    for ch in range(tm // SGU_CHUNK):
        rows = slice(ch * SGU_CHUNK, (ch + 1) * SGU_CHUNK)
        for c in range(SGU_WIDTH // LANES):
            sl = slice(c * LANES, (c + 1) * LANES)
            vs = vgn[rows, sl]
            r_lo = jnp.dot(w_heads[SLAB_HEADS * c], vs, preferred_element_type=F32)
            r_hi = jnp.dot(w_heads[SLAB_HEADS * c + 1], vs, preferred_element_type=F32)
            mixed = jnp.where(low_half, r_lo, r_hi) + sb_ref[:, sl]
            o = u[rows, sl] * mixed
            g_ref[rows, sl] = _group_rms(o, onw_ref[:, sl], low_half).astype(BF16)


def _in_proj(x2, nw, win, cos_t, sin_t, lnw, lnb, ws, sb, onw, *, seq, tm, tk):
    T = x2.shape[0]
    n_seq_tiles = seq // tm
    full = lambda shape: pl.BlockSpec(shape, lambda i: (0,) * len(shape))
    out = jax.ShapeDtypeStruct((T, DIFF_WIDTH), BF16)
    row_tile = pl.BlockSpec((tm, DIFF_WIDTH), lambda i: (i, 0))
    return pl.pallas_call(
        functools.partial(_in_proj_kernel, tm=tm, tk=tk),
        grid=(T // tm,),
        in_specs=[
            pl.BlockSpec((tm, D_MODEL), lambda i: (i, 0)),
            full((1, D_MODEL)),
            full((D_MODEL, IN_COLS)),
            pl.BlockSpec((tm, LANES), lambda i: (i % n_seq_tiles, 0)),
            pl.BlockSpec((tm, LANES), lambda i: (i % n_seq_tiles, 0)),
            full((1, SGU_WIDTH)),
            full((1, SGU_WIDTH)),
            full((N_SGU_HEADS, SGU_CHUNK, SGU_CHUNK)),
            full((SGU_CHUNK, SGU_WIDTH)),
            full((1, SGU_WIDTH)),
        ],
        out_specs=[row_tile, row_tile, pl.BlockSpec((tm // tk, DIFF_WIDTH, tk), lambda i: (i, 0, 0)), row_tile],
        out_shape=[out, out, jax.ShapeDtypeStruct((T // tk, DIFF_WIDTH, tk), BF16), out],
        compiler_params=pltpu.CompilerParams(dimension_semantics=("arbitrary",),
                                             vmem_limit_bytes=V7X_VMEM_LIMIT_BYTES),
        name="in_proj",
    )(x2, nw, win, cos_t, sin_t, lnw, lnb, ws, sb, onw)


def _attn_kernel(q_ref, k_ref, vt_ref, lam_ref, sw_ref, o_ref, m_ref, acc_ref, sa_ref, sb_ref,
                 *, tq, lam_init):
    qi = pl.program_id(1)
    lv = lam_ref[...]
    lam = (jnp.exp(jnp.sum(lv[0:1] * lv[1:2], axis=-1, keepdims=True))
           - jnp.exp(jnp.sum(lv[2:3] * lv[3:4], axis=-1, keepdims=True)) + lam_init)

    lane = lax.broadcasted_iota(jnp.int32, (tq, LANES), 1)
    key_pos = lax.broadcasted_iota(jnp.int32, (tq, tq), 0)
    qry_pos = lax.broadcasted_iota(jnp.int32, (tq, tq), 1)
    causal = key_pos <= qry_pos
    per_slab = 2 * SLAB_HEADS
    n_var = ATTN_SLABS_PER_PASS * per_slab
    nt = (((1,), (1,)), ((), ()))

    for c0 in range(0, N_SLABS, ATTN_SLABS_PER_PASS):
        slabs = [slice((c0 + u) * LANES, (c0 + u + 1) * LANES) for u in range(ATTN_SLABS_PER_PASS)]
        q_var = []
        for sl in slabs:
            qs = q_ref[:, sl]
            zero = jnp.zeros_like(qs)
            q_var += [jnp.where((lane >= x * DIFF_QK_DIM) & (lane < (x + 1) * DIFF_QK_DIM), qs, zero)
                      for x in range(per_slab)]
        m_ref[...] = jnp.full(m_ref.shape, NEG_INF, F32)
        acc_ref[...] = jnp.zeros(acc_ref.shape, F32)

        def score_tile(j, s_ref):
            rows = pl.ds(pl.multiple_of(j * tq, tq), tq)
            kts = [k_ref[rows, sl] for sl in slabs]
            for x in range(n_var):
                s_ref[x] = lax.dot_general(kts[x // per_slab], q_var[x], nt,
                                           preferred_element_type=F32)

        def softmax_pv(j, s_ref, diagonal):
            ones = jnp.ones((DENOM_ROWS, tq), BF16)
            vts = [jnp.concatenate([vt_ref[j, sl.start + hd * HEAD_DIM:sl.start + (hd + 1) * HEAD_DIM, :], ones],
                                   axis=0)
                   for sl in slabs for hd in range(SLAB_HEADS)]
            probs, alphas = [], []
            for x in range(n_var):
                s = s_ref[x]
                if diagonal:
                    s = jnp.where(causal, s, NEG_INF)
                m_prev = m_ref[x]
                m_next = jnp.maximum(m_prev, jnp.max(s, axis=0, keepdims=True))
                alpha = jnp.exp2(m_prev - m_next)
                p = jnp.exp2(s - m_next)
                m_ref[x] = m_next
                probs.append(p.astype(BF16))
                alphas.append(alpha)
            for x in range(n_var):
                acc_ref[x] = alphas[x] * acc_ref[x] + jnp.dot(vts[x // 2], probs[x],
                                                              preferred_element_type=F32)

        score_tile(0, sa_ref)

        def two_tiles(t, carry):
            j = 2 * t
            score_tile(j + 1, sb_ref)
            softmax_pv(j, sa_ref, False)
            score_tile(j + 2, sa_ref)
            softmax_pv(j + 1, sb_ref, False)
            return carry

        lax.fori_loop(0, qi // 2, two_tiles, 0)

        @pl.when(qi % 2 == 1)
        def _odd_tail():
            score_tile(qi, sb_ref)
            softmax_pv(qi - 1, sa_ref, False)
            softmax_pv(qi, sb_ref, True)

        @pl.when(qi % 2 == 0)
        def _even_tail():
            softmax_pv(qi, sa_ref, True)

        for u, sl in enumerate(slabs):
            x0 = u * per_slab
            part = [acc_ref[x0 + v, 0:HEAD_DIM, :] / acc_ref[x0 + v, HEAD_DIM:HEAD_DIM + 1, :]
                    for v in range(per_slab)]
            heads = []
            for hd in range(SLAB_HEADS):
                o = part[2 * hd] - lam * part[2 * hd + 1]
                ms = jnp.mean(o * o, axis=0, keepdims=True)
                heads.append(o * lax.rsqrt(ms + NORM_EPS))
            o = jnp.concatenate(heads, axis=0) * (sw_ref[...] * (1.0 - lam_init))
            o_ref[:, sl] = o.T.astype(BF16)


def _attention(q, k, vt, lam_vecs, sw, *, batch, seq, tq, lam_init):
    T = q.shape[0]
    nq = seq // tq
    n_var = ATTN_SLABS_PER_PASS * 2 * SLAB_HEADS
    assert vt.shape == (T // tq, DIFF_WIDTH, tq)
    return pl.pallas_call(
        functools.partial(_attn_kernel, tq=tq, lam_init=lam_init),
        grid=(batch, nq),
        in_specs=[
            pl.BlockSpec((tq, DIFF_WIDTH), lambda b, i: (b * nq + i, 0)),
            pl.BlockSpec((seq, DIFF_WIDTH), lambda b, i: (b, 0)),
            pl.BlockSpec((nq, DIFF_WIDTH, tq), lambda b, i: (b, 0, 0)),
            pl.BlockSpec((4, DIFF_QK_DIM), lambda b, i: (0, 0)),
            pl.BlockSpec((LANES, 1), lambda b, i: (0, 0)),
        ],
        out_specs=pl.BlockSpec((tq, DIFF_WIDTH), lambda b, i: (b * nq + i, 0)),
        out_shape=jax.ShapeDtypeStruct((T, DIFF_WIDTH), BF16),
        scratch_shapes=[pltpu.VMEM((n_var, 1, tq), F32),
                        pltpu.VMEM((n_var, HEAD_DIM + DENOM_ROWS, tq), F32), pltpu.VMEM((n_var, tq, tq), F32),
                        pltpu.VMEM((n_var, tq, tq), F32)],
        compiler_params=pltpu.CompilerParams(dimension_semantics=("arbitrary", "arbitrary"),
                                             vmem_limit_bytes=V7X_VMEM_LIMIT_BYTES),
        name="diff_attn",
    )(q, k, vt, lam_vecs, sw)


def _top16(s, key_idx, break_ties):
    cur = s
    rank = jnp.full(s.shape, NOT_SELECTED, F32)
    vals = jnp.zeros((PEER_TOPK, s.shape[1]), F32)
    slot = lax.broadcasted_iota(jnp.int32, vals.shape, 0)
    for it in range(PEER_TOPK):
        mx = jnp.max(cur, axis=0, keepdims=True)
        sel = cur == mx
        if break_ties:
            first = jnp.min(jnp.where(sel, key_idx, float(PEER_N_KEYS)), axis=0, keepdims=True)
            sel = key_idx == first
        cur = jnp.where(sel, NEG_INF, cur)
        rank = jnp.where(sel, float(it), rank)
        vals = jnp.where(slot == it, mx, vals)
    return rank, vals, cur == NEG_INF


_CAND_ROWS = 16 + 7 * 8 + 8


def _cand_flat_index():
    r = lax.broadcasted_iota(jnp.int32, (_CAND_ROWS, LANES), 0)
    a = jnp.where(r < 16, 0, jnp.where(r < 72, 1 + ((r - 16) >> 3), 8 + (r - 72)))
    b = jnp.where(r < 16, r, jnp.where(r < 72, (r - 16) & 7, 0))
    return (a * PEER_TOPK + b).astype(F32)


def _select_pairs(v1, v2, flat, break_ties):
    pieces = [v1[0:1] + v2]
    pieces += [v1[a:a + 1] + v2[0:8] for a in range(1, 8)]
    pieces += [v1[8:16] + v2[0:1]]
    cand = jnp.concatenate(pieces, axis=0)
    z = jnp.zeros((1, cand.shape[1]), F32)
    top = None
    for it in range(PEER_TOPK):
        mx = jnp.max(cand, axis=0, keepdims=True)
        if it == 0:
            top = mx
        sel = cand == mx
        if break_ties:
            first = jnp.min(jnp.where(sel, flat, 1e9), axis=0, keepdims=True)
            sel = flat == first
        cand = jnp.where(sel, NEG_INF, cand)
        z = z + jnp.exp(mx - top)
    picked = jnp.where(cand == NEG_INF, 1.0, 0.0)
    counts = [jnp.sum(picked[0:16], axis=0, keepdims=True)]
    counts += [jnp.sum(picked[16 + 8 * (a - 1):24 + 8 * (a - 1)], axis=0, keepdims=True) for a in range(1, 8)]
    counts += [picked[72 + a:73 + a] for a in range(8)]
    return counts, z


def _peer_kernel(x_ref, at_ref, sg_ref, wa_ref, wg_ref, nw_ref, wq_ref, keys_ref, u_ref, vt_ref, fw_ref, o_ref,
                 hn_ref, q_ref, l_ref, nb_ref, r_ref, rk_ref, w_ref, h_ref, y_ref, *, tm, te):
    g = pl.program_id(1)
    n_chunks = tm // LANES
    n_tok = tm // PEER_TOK
    chunks_per_tok = PEER_TOK // LANES
    i_per_tile = te // PEER_N_KEYS
    nt = (((1,), (1,)), ((), ()))

    @pl.when(g == 0)
    def _select():
        y_ref[...] = jnp.zeros(y_ref.shape, F32)
        x1 = (x_ref[...] + jnp.dot(at_ref[...], wa_ref[...], preferred_element_type=F32)
              + jnp.dot(sg_ref[...], wg_ref[...], preferred_element_type=F32))
        o_ref[...] = x1
        hn_ref[...] = _rms(x1, nw_ref[...]).astype(BF16)
        q = jnp.dot(hn_ref[...], wq_ref[...], preferred_element_type=F32).astype(BF16)
        for hh in range(PEER_HEADS):
            q_ref[hh] = q[:, hh * PEER_QUERY_DIM:(hh + 1) * PEER_QUERY_DIM]
        key_idx = lax.broadcasted_iota(jnp.int32, (PEER_N_KEYS, LANES), 0).astype(F32)
        flat = _cand_flat_index()

        def select_chunk(hh, lc, break_ties):
            rows = pl.ds(pl.multiple_of(lc * LANES, LANES), LANES)
            s1 = lax.dot_general(keys_ref[0], q_ref[hh, rows, 0:PEER_HALF], nt,
                                 preferred_element_type=F32)
            s2 = lax.dot_general(keys_ref[1], q_ref[hh, rows, PEER_HALF:PEER_QUERY_DIM], nt,
                                 preferred_element_type=F32)
            rank1, v1, in1 = _top16(s1, key_idx, break_ties)
            rank2, v2, in2 = _top16(s2, key_idx, break_ties)
            counts, z = _select_pairs(v1, v2, flat, break_ties)
            nb = jnp.zeros(s1.shape, F32)
            for a in range(PEER_TOPK):
                at_round = (rank1 == float(a)) if break_ties else (s1 == v1[a:a + 1])
                nb = jnp.where(at_round, counts[a], nb)
            nb_ref[hh, lc] = nb
            rk_ref[hh, lc] = rank2
            l_ref[hh, lc] = jnp.where(in1, 0.5 * jnp.exp(s1 - v1[0:1]), 0.0)
            r_ref[hh, lc] = jnp.where(in2, jnp.exp(s2 - v2[0:1]), 0.0) / z
            taken = [jnp.sum(jnp.where(in1, 1.0, 0.0), axis=0, keepdims=True),
                     jnp.sum(jnp.where(in2, 1.0, 0.0), axis=0, keepdims=True), sum(counts)]
            return sum(jnp.sum(jnp.where(t == float(PEER_TOPK), 0.0, 1.0)) for t in taken)

        def select(idx, carry):
            hh = idx // (n_chunks // SELECT_UNROLL)
            lc0 = (idx % (n_chunks // SELECT_UNROLL)) * SELECT_UNROLL
            tokens_with_ties = sum(select_chunk(hh, lc0 + k, False) for k in range(SELECT_UNROLL))

            @pl.when(tokens_with_ties > 0.0)
            def _redo_in_top_k_order():
                for k in range(SELECT_UNROLL):
                    select_chunk(hh, lc0 + k, True)

            return carry

        lax.fori_loop(0, PEER_HEADS * n_chunks // SELECT_UNROLL, select, 0)

    i0 = pl.multiple_of(g * i_per_tile, i_per_tile)

    @pl.when(g >= 0)
    def _gate():
        for lc in range(n_chunks):
            th, cl = divmod(lc, chunks_per_tok)
            lanes = slice(cl * LANES, (cl + 1) * LANES)
            for ii in range(i_per_tile):
                w = None
                for hh in range(PEER_HEADS):
                    nb_rows = nb_ref[hh, lc, pl.ds(i0, i_per_tile), :]
                    l_rows = l_ref[hh, lc, pl.ds(i0, i_per_tile), :]
                    term = (jnp.where(rk_ref[hh, lc] < nb_rows[ii:ii + 1], r_ref[hh, lc], 0.0)
                            * l_rows[ii:ii + 1])
                    w = term if w is None else w + term
                w_ref[th, ii * PEER_N_KEYS:(ii + 1) * PEER_N_KEYS, lanes] = w

    @pl.when(g >= -1)
    def _activations():
        for th in range(n_tok):
            a = lax.dot_general(u_ref[...], hn_ref[th * PEER_TOK:(th + 1) * PEER_TOK, :], nt,
                                preferred_element_type=F32)
            h_ref[th] = (a * (1.0 + lax.erf(a * SQRT_HALF)) * w_ref[th]).astype(BF16)

    @pl.when(g >= -2)
    def _values():
        for th in range(n_tok):
            y_ref[:, th * PEER_TOK:(th + 1) * PEER_TOK] += jnp.dot(vt_ref[0], h_ref[th],
                                                                   preferred_element_type=F32)

    @pl.when(g == pl.num_programs(1) - 1)
    def _finish():
        x2 = o_ref[...] + y_ref[...].T
        o_ref[...] = _rms(x2, fw_ref[...])


def _out_proj_peer(x2, attn, sgu, wa, wg, nw, wq, keys, u_tab, vt_tiles, fw, *, tm):
    T = x2.shape[0]
    te = PEER_TILE
    assert te % (8 * PEER_N_KEYS) == 0
    assert vt_tiles.shape == (PEER_N_EXPERTS // te, D_MODEL, te)
    n_chunks = tm // LANES
    n_tok = tm // PEER_TOK
    full = lambda shape: pl.BlockSpec(shape, lambda i, g: (0,) * len(shape), pipeline_mode=pl.Buffered(1))
    tok_tile = lambda width: pl.BlockSpec((tm, width), lambda i, g: (i, 0))
    first_step_tile = lambda width: pl.BlockSpec((tm, width), lambda i, g: (i, 0), pipeline_mode=pl.Buffered(1))
    sel_scratch = pltpu.VMEM((PEER_HEADS, n_chunks, PEER_N_KEYS, LANES), F32)
    return pl.pallas_call(
        functools.partial(_peer_kernel, tm=tm, te=te),
        grid=(T // tm, PEER_N_EXPERTS // te),
        in_specs=[
            tok_tile(D_MODEL), first_step_tile(DIFF_WIDTH), first_step_tile(SGU_WIDTH),
            full((DIFF_WIDTH, D_MODEL)),
            full((SGU_WIDTH, D_MODEL)),
            full((1, D_MODEL)),
            full((D_MODEL, PEER_HEADS * PEER_QUERY_DIM)),
            full((2, PEER_N_KEYS, PEER_HALF)),
            pl.BlockSpec((te, D_MODEL), lambda i, g: (g, 0)),
            pl.BlockSpec((1, D_MODEL, te), lambda i, g: (g, 0, 0)),
            full((1, D_MODEL)),
        ],
        out_specs=tok_tile(D_MODEL),
        out_shape=jax.ShapeDtypeStruct((T, D_MODEL), F32),
        scratch_shapes=[pltpu.VMEM((tm, D_MODEL), BF16),
                        pltpu.VMEM((PEER_HEADS, tm, PEER_QUERY_DIM), BF16),
                        sel_scratch, sel_scratch, sel_scratch, sel_scratch,
                        pltpu.VMEM((n_tok, te, PEER_TOK), F32), pltpu.VMEM((n_tok, te, PEER_TOK), BF16),
                        pltpu.VMEM((D_MODEL, tm), F32)],
        compiler_params=pltpu.CompilerParams(dimension_semantics=("arbitrary", "arbitrary"),
                                             vmem_limit_bytes=V7X_VMEM_LIMIT_BYTES),
        name="out_proj_peer",
    )(x2, attn, sgu, wa, wg, nw, wq, keys, u_tab, vt_tiles, fw)


def _rope_tables(seq):
    pos = jnp.arange(seq, dtype=F32)
    inv_freq = ROPE_THETA ** (-jnp.arange(0, DIFF_QK_DIM, 2, dtype=F32) / DIFF_QK_DIM)
    ang = pos[:, None] * inv_freq[None, :]
    cos = jnp.cos(ang)
    sin = jnp.sin(ang)
    reps = LANES // DIFF_QK_DIM
    cos_t = jnp.tile(jnp.concatenate([cos, cos], axis=-1), (1, reps))
    sin_t = jnp.tile(jnp.concatenate([-sin, sin], axis=-1), (1, reps))
    return cos_t, sin_t


def _tiles(batch, seq):
    tm = min(512, seq)
    tq = min(256, seq)
    t_peer = min(512, batch * seq)
    return tm, tq, t_peer


def kernel(x, attn_norm_w, w_in, lambda_q1, lambda_k1, lambda_q2, lambda_k2, subln_w, sgu_ln_w, sgu_ln_b,
           sgu_ws, sgu_b, sgu_out_norm_w, w_out, ffn_norm_w, peer_wq, peer_keys, peer_u, peer_v, final_norm_w):
    batch, seq, d_model = x.shape
    assert d_model == D_MODEL and seq % SGU_CHUNK == 0
    assert w_in.shape[0] == 1, "single-layer block: the PEER call folds in the closing RMSNorm"
    tm, tq, t_peer = _tiles(batch, seq)
    assert seq % tm == 0 and tm % tq == 0 and (batch * seq) % t_peer == 0
    cos_t, sin_t = _rope_tables(seq)
    x2 = x.reshape(batch * seq, D_MODEL)
    row = lambda w: w.reshape(1, -1)
    l = 0
    lam_init = 0.8 - 0.6 * math.exp(-0.3 * l)
    sb = jnp.repeat(sgu_b[l].T, HEAD_DIM, axis=1)
    q, k, vt, sgu = _in_proj(x2, row(attn_norm_w[l]), w_in[l].astype(BF16), cos_t, sin_t,
                             row(sgu_ln_w[l]), row(sgu_ln_b[l]), sgu_ws[l], sb, row(sgu_out_norm_w[l]),
                             seq=seq, tm=tm, tk=tq)
    lam_vecs = jnp.stack([lambda_q1[l], lambda_k1[l], lambda_q2[l], lambda_k2[l]])
    attn = _attention(q, k, vt, lam_vecs, jnp.tile(subln_w[l], SLAB_HEADS).reshape(LANES, 1),
                      batch=batch, seq=seq, tq=tq, lam_init=lam_init)
    wo = w_out[l].astype(BF16)
    vt_tiles = peer_v[l].astype(BF16).reshape(-1, PEER_TILE, D_MODEL).transpose(0, 2, 1)
    out = _out_proj_peer(x2, attn, sgu, wo[:DIFF_WIDTH], wo[DIFF_WIDTH:], row(ffn_norm_w[l]),
                         peer_wq[l].astype(BF16), peer_keys[l].astype(BF16), peer_u[l].astype(BF16),
                         vt_tiles, row(final_norm_w), tm=t_peer)
    return out.reshape(batch, seq, D_MODEL)
```
